```python
import jax
import jax.numpy as jnp
from jax import lax
import numpy as np


D_MODEL = 2048
BATCH = 2
SEQ = 8192
DEPTH = 4

D_FF = 5504
NORM_EPS = 1e-6
ROPE_THETA = 10000.0
N_AB = (DEPTH + 1) // 2
N_C = DEPTH // 2
NEG = -1e30

ML_HEADS = 4
ML_HEAD_DIM = D_MODEL // 2 // ML_HEADS
ML_WIDTH = ML_HEADS * ML_HEAD_DIM
ML_CONV = 4
ML_CHUNK = 128

HG_HEADS = 8
HG_KEY_DIM = 128
HG_VAL_DIM = D_MODEL // 2 // HG_HEADS
HG_KEY_WIDTH = HG_HEADS * HG_KEY_DIM
HG_WIDTH = HG_HEADS * HG_VAL_DIM
HG_CHUNK = 64
HG_MAX_K = 0.999999

AB_SIZES = [ML_WIDTH, ML_WIDTH, ML_WIDTH, ML_HEADS, ML_HEADS, HG_KEY_WIDTH, HG_KEY_WIDTH, HG_WIDTH, HG_WIDTH]
AB_COLS = sum(AB_SIZES)

NSA_HEADS = 16
NSA_KV_GROUPS = 4
NSA_HEAD_DIM = D_MODEL // NSA_HEADS
NSA_KV_WIDTH = NSA_KV_GROUPS * NSA_HEAD_DIM
CMP_BLOCK = 32
CMP_STRIDE = 16
CMP_HIDDEN = 256
SEL_BLOCK = 64
SEL_TOPK = 16
WINDOW = 512
NSA_QBLOCK = 32
FORCE_BONUS = 1e4
C_SIZES = [NSA_HEADS * NSA_HEAD_DIM] + [NSA_KV_WIDTH] * 6 + [3 * NSA_HEADS]
C_COLS = sum(C_SIZES)

kernel_name = 'hybrid_mlstm_hgrn2_nsa_macaron'


def _split(a, sizes):
    offs = np.cumsum(sizes)[:-1].tolist()
    return jnp.split(a, offs, axis=-1)


def rmsnorm(x, w):
    xf = x.astype(jnp.float32)
    y = xf * lax.rsqrt(jnp.mean(xf * xf, axis=-1, keepdims=True) + NORM_EPS)
    return (y * w.astype(jnp.float32)).astype(x.dtype)


def swiglu(x, w_gate, w_up, w_down):
    return (jax.nn.silu(x @ w_gate) * (x @ w_up)) @ w_down


def rope(x, pos):
    half = x.shape[-1] // 2
    inv_freq = jnp.power(ROPE_THETA, -jnp.arange(half, dtype=jnp.float32) / half)
    ang = pos.astype(jnp.float32)[:, None] * inv_freq[None, :]
    cos, sin = jnp.cos(ang), jnp.sin(ang)
    xf = x.astype(jnp.float32)
    x1, x2 = xf[..., :half], xf[..., half:]
    return jnp.concatenate([x1 * cos - x2 * sin, x2 * cos + x1 * sin], axis=-1).astype(x.dtype)


def masked_softmax(s, valid):
    s = jnp.where(valid, s.astype(jnp.float32), NEG)
    m = jnp.max(s, axis=-1, keepdims=True)
    e = jnp.where(valid, jnp.exp(s - m), 0.0)
    z = jnp.sum(e, axis=-1, keepdims=True)
    return e / jnp.where(z > 0, z, 1.0)


def causal_dwconv(u, w, b):
    K, C = w.shape
    y = lax.conv_general_dilated(u, w[:, None, :].astype(u.dtype), window_strides=(1,), padding=[(K - 1, 0)],
                                 dimension_numbers=('NWC', 'WIO', 'NWC'), feature_group_count=C)
    return y + b


def mlstm_chunkwise(q, k, v, i_pre, f_pre):
    B, H, T, dh = q.shape
    L = ML_CHUNK
    nc = T // L
    f32 = jnp.float32
    out_dtype = q.dtype
    q, k, v = (a.astype(f32) for a in (q, k, v))
    log_f = jax.nn.log_sigmoid(f_pre.astype(f32))
    i_pre = i_pre.astype(f32)

    def chunks(a):
        return jnp.moveaxis(a.reshape(B, H, nc, L, *a.shape[3:]), 2, 0)

    causal = jnp.tril(jnp.ones((L, L), dtype=bool))

    def step(carry, xs):
        C, n, m = carry
        qj, kj, vj, ij, fj = xs
        b = jnp.cumsum(fj, axis=-1)
        dlog = jnp.where(causal, b[..., :, None] - b[..., None, :] + ij[..., None, :], NEG)
        inter = b + m[..., None]
        m_t = jnp.maximum(inter, jnp.max(dlog, axis=-1))
        w_intra = jnp.exp(dlog - m_t[..., None])
        w_inter = jnp.exp(inter - m_t)
        qk = jnp.einsum('bhld,bhsd->bhls', qj, kj) * w_intra
        num = jnp.einsum('bhls,bhsd->bhld', qk, vj) + w_inter[..., None] * jnp.einsum('bhvk,bhlk->bhlv', C, qj)
        den = jnp.sum(qk, axis=-1) + w_inter * jnp.einsum('bhk,bhlk->bhl', n, qj)
        h = num / jnp.maximum(jnp.abs(den), jnp.exp(-m_t))[..., None]
        g = b[..., -1]
        upd = g[..., None] - b + ij
        m_new = jnp.maximum(g + m, jnp.max(upd, axis=-1))
        a_prev = jnp.exp(g + m - m_new)
        a_s = jnp.exp(upd - m_new[..., None])
        C = a_prev[..., None, None] * C + jnp.einsum('bhs,bhsv,bhsk->bhvk', a_s, vj, kj)
        n = a_prev[..., None] * n + jnp.einsum('bhs,bhsk->bhk', a_s, kj)
        return (C, n, m_new), h

    init = (jnp.zeros((B, H, dh, dh), f32), jnp.zeros((B, H, dh), f32), jnp.zeros((B, H), f32))
    _, h = lax.scan(step, init, (chunks(q), chunks(k), chunks(v), chunks(i_pre), chunks(log_f)))
    return jnp.moveaxis(h, 0, 2).reshape(B, H, T, dh).astype(out_dtype)


def hgrn2_chunkwise(q, k, v, log_f):
    B, H, T, dk = q.shape
    dv = v.shape[-1]
    L = HG_CHUNK
    nc = T // L
    f32 = jnp.float32

    def chunks(a):
        return jnp.moveaxis(a.astype(f32).reshape(B, H, nc, L, a.shape[-1]), 2, 0)

    causal = jnp.tril(jnp.ones((L, L), dtype=bool))[:, :, None]

    def step(S, xs):
        qj, kj, vj, fj = xs
        b = jnp.cumsum(fj, axis=2)
        decay = jnp.exp(jnp.where(causal, b[:, :, :, None, :] - b[:, :, None, :, :], NEG))
        att = jnp.einsum('bhlc,bhlsc,bhsc->bhls', qj, decay, kj)
        o = jnp.einsum('bhls,bhsv->bhlv', att, vj) + jnp.einsum('bhlc,bhcv->bhlv', qj * jnp.exp(b), S)
        g = b[:, :, -1:, :]
        S = jnp.exp(g[:, :, 0, :])[..., None] * S + jnp.einsum('bhsc,bhsv->bhcv', kj * jnp.exp(g - b), vj)
        return S, o

    S0 = jnp.zeros((B, H, dk, dv), f32)
    _, o = lax.scan(step, S0, (chunks(q), chunks(k), chunks(v), chunks(log_f)))
    return jnp.moveaxis(o, 0, 2).reshape(B, H, T, dv).astype(v.dtype)


def ab_mixer(h, w_in, w_out, conv_w, conv_b, wq, wk, i_bias, f_bias, ml_norm, ml_skip, lb, hg_norm):
    B, T, _ = h.shape
    u, v, o_pre, i_pre, f_pre, hq, hf, hi, hg = _split(h @ w_in, AB_SIZES)
    c = jax.nn.silu(causal_dwconv(u, conv_w, conv_b))
    ch = c.reshape(B, T, ML_HEADS, ML_HEAD_DIM)
    q = jnp.einsum('bthd,hde->bhte', ch, wq)
    k = jnp.einsum('bthd,hde->bhte', ch, wk) * ML_HEAD_DIM ** -0.5
    vm = v.reshape(B, T, ML_HEADS, ML_HEAD_DIM).transpose(0, 2, 1, 3)
    hm = mlstm_chunkwise(q, k, vm, (i_pre + i_bias).transpose(0, 2, 1), (f_pre + f_bias).transpose(0, 2, 1))
    hm = rmsnorm(hm.transpose(0, 2, 1, 3), ml_norm).reshape(B, T, ML_WIDTH)
    y_ml = jax.nn.sigmoid(o_pre) * (hm + ml_skip * c)
    lbf = lb.astype(jnp.float32)
    zf = hf.astype(jnp.float32)
    k2 = (1.0 - lbf) * jax.nn.sigmoid(-zf)
    log_f = jnp.maximum(jnp.log1p(-jnp.minimum(k2, HG_MAX_K)), jax.nn.log_sigmoid(zf))

    def to_heads(a, n):
        return a.reshape(B, T, HG_HEADS, n).transpose(0, 2, 1, 3)

    o2 = hgrn2_chunkwise(to_heads(jax.nn.silu(hq), HG_KEY_DIM), to_heads(k2, HG_KEY_DIM),
                         to_heads(hi, HG_VAL_DIM), to_heads(log_f, HG_KEY_DIM))
    o2 = rmsnorm(o2.transpose(0, 2, 1, 3), hg_norm).reshape(B, T, HG_WIDTH)
    y_hg = o2.astype(h.dtype) * jax.nn.silu(hg)
    return jnp.concatenate([y_ml.astype(h.dtype), y_hg], axis=-1) @ w_out


def compress_blocks(kv, pos_emb, w1, b1, w2):
    B, G, T, d = kv.shape
    nc = (T - CMP_BLOCK) // CMP_STRIDE + 1
    idx = jnp.arange(nc)[:, None] * CMP_STRIDE + jnp.arange(CMP_BLOCK)[None, :]
    blocks = kv[:, :, idx, :] + pos_emb
    flat = blocks.reshape(B, G, nc, CMP_BLOCK * d)
    return jax.nn.silu(flat @ w1 + b1) @ w2


def nsa_attention(q, k_cmp, v_cmp, k_sel, v_sel, k_win, v_win, gates):
    B, G, R, T, d = q.shape
    f32 = jnp.float32
    nc = k_cmp.shape[2]
    ns = T // SEL_BLOCK
    n_sel = min(SEL_TOPK, ns)
    scale = d ** -0.5
    QB = NSA_QBLOCK
    cmp_end = jnp.arange(nc) * CMP_STRIDE + CMP_BLOCK - 1
    ci = jnp.arange(nc)[:, None] * CMP_STRIDE
    sj = jnp.arange(ns)[None, :] * SEL_BLOCK
    cover = ((ci < sj + SEL_BLOCK) & (ci + CMP_BLOCK > sj)).astype(f32)
    ksb = k_sel.reshape(B, G, ns, SEL_BLOCK, d)
    vsb = v_sel.reshape(B, G, ns, SEL_BLOCK, d)
    pad = ((0, 0), (0, 0), (WINDOW, 0), (0, 0))
    kwp, vwp = jnp.pad(k_win, pad), jnp.pad(v_win, pad)
    blk_ids = jnp.arange(ns)
    take = jax.vmap(jax.vmap(lambda blocks, ix: blocks[ix]))

    def one_block(bi):
        q0 = bi * QB
        t = q0 + jnp.arange(QB)
        qb = lax.dynamic_slice_in_dim(q, q0, QB, axis=3)
        gb = lax.dynamic_slice_in_dim(gates, q0, QB, axis=3).astype(f32)
        s_c = jnp.einsum('bgrqd,bgnd->bgrqn', qb, k_cmp, preferred_element_type=f32) * scale
        p_c = masked_softmax(s_c, cmp_end[None, :] <= t[:, None])
        o_c = jnp.einsum('bgrqn,bgnd->bgrqd', p_c, v_cmp.astype(f32))
        imp = jnp.einsum('bgrqn,ns->bgqs', p_c, cover)
        cur = (t // SEL_BLOCK)[:, None]
        forced = (blk_ids[None, :] == 0) | (blk_ids[None, :] == cur) | (blk_ids[None, :] == cur - 1)
        imp = jnp.where(blk_ids[None, :] * SEL_BLOCK <= t[:, None], imp + jnp.where(forced, FORCE_BONUS, 0.0), NEG)
        _, idx = lax.top_k(imp, n_sel)
        kg, vg = take(ksb, idx), take(vsb, idx)
        s_s = jnp.einsum('bgrqd,bgqnkd->bgrqnk', qb, kg, preferred_element_type=f32) * scale
        key_pos = idx[..., None] * SEL_BLOCK + jnp.arange(SEL_BLOCK)
        valid_s = (key_pos <= t[:, None, None])[:, :, None]
        p_s = masked_softmax(s_s.reshape(B, G, R, QB, -1), valid_s.reshape(B, G, 1, QB, -1)).reshape(s_s.shape)
        o_s = jnp.einsum('bgrqnk,bgqnkd->bgrqd', p_s, vg.astype(f32))
        kw = lax.dynamic_slice_in_dim(kwp, q0, WINDOW + QB, axis=2)
        vw = lax.dynamic_slice_in_dim(vwp, q0, WINDOW + QB, axis=2)
        kpos = q0 - WINDOW + jnp.arange(WINDOW + QB)
        valid_w = (kpos[None, :] >= 0) & (kpos[None, :] <= t[:, None]) & (kpos[None, :] > t[:, None] - WINDOW)
        s_w = jnp.einsum('bgrqd,bgkd->bgrqk', qb, kw, preferred_element_type=f32) * scale
        p_w = masked_softmax(s_w, valid_w)
        o_w = jnp.einsum('bgrqk,bgkd->bgrqd', p_w, vw.astype(f32))
        return gb[..., 0:1] * o_c + gb[..., 1:2] * o_s + gb[..., 2:3] * o_w

    out = lax.map(one_block, jnp.arange(T // QB))
    return jnp.moveaxis(out, 0, 3).reshape(B, G, R, T, d).astype(q.dtype)


def nsa_mixer(h, w_in, w_out, q_norm, k_norm, cmp_pos, cmp_w1, cmp_b1, cmp_w2, gate_bias):
    B, T, _ = h.shape
    d, G, R = NSA_HEAD_DIM, NSA_KV_GROUPS, NSA_HEADS // NSA_KV_GROUPS
    q, kc, vc, ks, vs, kw, vw, gp = _split(h @ w_in, C_SIZES)

    def heads(a, n):
        return a.reshape(B, T, n, d).transpose(0, 2, 1, 3)

    pos = jnp.arange(T)
    q = rope(rmsnorm(heads(q, NSA_HEADS), q_norm), pos).reshape(B, G, R, T, d)
    k_sel = rope(rmsnorm(heads(ks, G), k_norm[1]), pos)
    k_win = rope(rmsnorm(heads(kw, G), k_norm[2]), pos)
    k_cmp = compress_blocks(heads(kc, G), cmp_pos[0], cmp_w1[0], cmp_b1[0], cmp_w2[0])
    nc = k_cmp.shape[2]
    k_cmp = rope(rmsnorm(k_cmp, k_norm[0]), jnp.arange(nc) * CMP_STRIDE + CMP_BLOCK - 1)
    v_cmp = compress_blocks(heads(vc, G), cmp_pos[1], cmp_w1[1], cmp_b1[1], cmp_w2[1])
    gates = jax.nn.sigmoid(gp + gate_bias).reshape(B, T, G, R, 3).transpose(0, 2, 3, 1, 4)
    o = nsa_attention(q, k_cmp, v_cmp, k_sel, heads(vs, G), k_win, heads(vw, G), gates)
    o = o.reshape(B, NSA_HEADS, T, d).transpose(0, 2, 1, 3).reshape(B, T, NSA_HEADS * d)
    return o @ w_out


def setup_inputs(seed: int = 0) -> dict:
    key = jax.random.key(seed)
    ks = jax.random.split(key, 40)
    ctr = [0]

    def nrm(shape, scale):
        kk = ks[ctr[0]]
        ctr[0] += 1
        return jax.random.normal(kk, shape, jnp.float32) * scale

    def gain(shape):
        return 1.0 + nrm(shape, 0.02)

    D, F = D_MODEL, D_FF
    d = NSA_HEAD_DIM
    return {
        'x': nrm((BATCH, SEQ, D), 1.0),
        'ffn1_norm': gain((DEPTH, D)),
        'ffn1_w_gate': nrm((DEPTH, D, F), D ** -0.5),
        'ffn1_w_up': nrm((DEPTH, D, F), D ** -0.5),
        'ffn1_w_down': nrm((DEPTH, F, D), F ** -0.5),
        'mix_norm': gain((DEPTH, D)),
        'ffn2_norm': gain((DEPTH, D)),
        'ffn2_w_gate': nrm((DEPTH, D, F), D ** -0.5),
        'ffn2_w_up': nrm((DEPTH, D, F), D ** -0.5),
        'ffn2_w_down': nrm((DEPTH, F, D), F ** -0.5),
        'ab_w_in': nrm((N_AB, D, AB_COLS), D ** -0.5),
        'ab_w_out': nrm((N_AB, ML_WIDTH + HG_WIDTH, D), (ML_WIDTH + HG_WIDTH) ** -0.5),
        'ml_conv_w': nrm((N_AB, ML_CONV, ML_WIDTH), ML_CONV ** -0.5),
        'ml_conv_b': nrm((N_AB, ML_WIDTH), 0.02),
        'ml_wq': nrm((N_AB, ML_HEADS, ML_HEAD_DIM, ML_HEAD_DIM), ML_HEAD_DIM ** -0.5),
        'ml_wk': nrm((N_AB, ML_HEADS, ML_HEAD_DIM, ML_HEAD_DIM), ML_HEAD_DIM ** -0.5),
        'ml_i_bias': nrm((N_AB, ML_HEADS), 0.1),
        'ml_f_bias': jnp.linspace(3.0, 6.0, ML_HEADS, dtype=jnp.float32) + nrm((N_AB, ML_HEADS), 0.1),
        'ml_out_norm': gain((N_AB, ML_HEADS, ML_HEAD_DIM)),
        'ml_skip': gain((N_AB, ML_WIDTH)),
        'hg_lb_logits': nrm((N_AB, HG_KEY_WIDTH), 0.5),
        'hg_out_norm': gain((N_AB, HG_HEADS, HG_VAL_DIM)),
        'c_w_in': nrm((N_C, D, C_COLS), D ** -0.5),
        'c_w_out': nrm((N_C, NSA_HEADS * d, D), (NSA_HEADS * d) ** -0.5),
        'c_q_norm': gain((N_C, d)),
        'c_k_norm': gain((N_C, 3, d)),
        'c_cmp_pos': nrm((N_C, 2, CMP_BLOCK, d), 0.1),
        'c_cmp_w1': nrm((N_C, 2, CMP_BLOCK * d, CMP_HIDDEN), (CMP_BLOCK * d) ** -0.5),
        'c_cmp_b1': nrm((N_C, 2, CMP_HIDDEN), 0.02),
        'c_cmp_w2': nrm((N_C, 2, CMP_HIDDEN, d), CMP_HIDDEN ** -0.5),
        'c_gate_bias': nrm((N_C, 3 * NSA_HEADS), 0.1),
    }


def reference(x, ffn1_norm, ffn1_w_gate, ffn1_w_up, ffn1_w_down, mix_norm, ffn2_norm, ffn2_w_gate, ffn2_w_up,
              ffn2_w_down, ab_w_in, ab_w_out, ml_conv_w, ml_conv_b, ml_wq, ml_wk, ml_i_bias, ml_f_bias, ml_out_norm,
              ml_skip, hg_lb_logits, hg_out_norm, c_w_in, c_w_out, c_q_norm, c_k_norm, c_cmp_pos, c_cmp_w1, c_cmp_b1,
              c_cmp_w2, c_gate_bias):
    lb_soft = jax.nn.softmax(hg_lb_logits.astype(jnp.float32), axis=0)
    lb_all = jnp.cumsum(lb_soft, axis=0) - lb_soft[0]
    h = x
    for layer in range(DEPTH):
        h = h + 0.5 * swiglu(rmsnorm(h, ffn1_norm[layer]), ffn1_w_gate[layer], ffn1_w_up[layer], ffn1_w_down[layer])
        hn = rmsnorm(h, mix_norm[layer])
        j = layer // 2
        if layer % 2 == 0:
            h = h + ab_mixer(hn, ab_w_in[j], ab_w_out[j], ml_conv_w[j], ml_conv_b[j], ml_wq[j], ml_wk[j],
                             ml_i_bias[j], ml_f_bias[j], ml_out_norm[j], ml_skip[j], lb_all[j], hg_out_norm[j])
        else:
            h = h + nsa_mixer(hn, c_w_in[j], c_w_out[j], c_q_norm[j], c_k_norm[j], c_cmp_pos[j], c_cmp_w1[j],
                              c_cmp_b1[j], c_cmp_w2[j], c_gate_bias[j])
        h = h + 0.5 * swiglu(rmsnorm(h, ffn2_norm[layer]), ffn2_w_gate[layer], ffn2_w_up[layer], ffn2_w_down[layer])
    return h
```

```python
import functools

import numpy as np
import jax
import jax.numpy as jnp
from jax import lax
from jax.experimental import pallas as pl
from jax.experimental.pallas import tpu as pltpu

F32 = jnp.float32
BF16 = jnp.bfloat16

D_MODEL = 2048
D_FF = 5504
NORM_EPS = 1e-6
ROPE_THETA = 10000.0
NEG = -1e30

ML_HEADS = 4
ML_HEAD_DIM = 256
ML_WIDTH = 1024
ML_CONV = 4

HG_HEADS = 8
HG_KEY_DIM = 128
HG_VAL_DIM = 128
HG_KEY_WIDTH = 1024
HG_WIDTH = 1024
HG_MAX_K = 0.999999

NSA_HEADS = 16
NSA_KV_GROUPS = 4
NSA_REP = NSA_HEADS // NSA_KV_GROUPS
NSA_HEAD_DIM = 128
NSA_KV_WIDTH = NSA_KV_GROUPS * NSA_HEAD_DIM
CMP_BLOCK = 32
CMP_STRIDE = 16
CMP_HIDDEN = 256
SEL_BLOCK = 64
SEL_TOPK = 16
WINDOW = 512
FORCE_BONUS = 1e4

LANES = 128
VMEM_LIMIT = 56 * 1024 * 1024

FFN_TM = 512
FFN_TF = 512
D_FF_PAD = 5632
PROJ_TM = 512
PROJ_TN = 512
ML_CHUNK = 128
HG_CHUNK = 128
HG_LEVELS = (8, 16, 32, 64)
NSA_TQ = 128
NSA_TK = 256


def _params(sem):
    return pltpu.CompilerParams(dimension_semantics=sem, vmem_limit_bytes=VMEM_LIMIT)


def _rms(x, w):
    ms = jnp.mean(x * x, axis=-1, keepdims=True)
    return x * lax.rsqrt(ms + NORM_EPS) * w


def _log_sigmoid(x):
    return jnp.minimum(x, 0.0) - jnp.log1p(jnp.exp(-jnp.abs(x)))


def _dot(a, b):
    return jnp.dot(a, b, preferred_element_type=F32)


def _dot_nt(a, b):
    return lax.dot_general(a, b, (((1,), (1,)), ((), ())), preferred_element_type=F32)


def _dot_tn(a, b):
    return lax.dot_general(a, b, (((0,), (0,)), ((), ())), preferred_element_type=F32)


def _split3(x):
    hi = x.astype(BF16)
    r1 = x - hi.astype(F32)
    mid = r1.astype(BF16)
    lo = (r1 - mid.astype(F32)).astype(BF16)
    return hi, mid, lo


def _ffn_body(h_ref, nw_ref, wg_ref, wu_ref, wd_ref, o_ref, hn_ref):
    @pl.when(pl.program_id(1) == 0)
    def _():
        x = h_ref[...]
        hn_ref[...] = _rms(x, nw_ref[...]).astype(BF16)
        o_ref[...] = x

    hn = hn_ref[...]
    g = _dot(hn, wg_ref[...])
    u = _dot(hn, wu_ref[...])
    a = (g * jax.nn.sigmoid(g) * u).astype(BF16)
    o_ref[...] += 0.5 * _dot(a, wd_ref[...])


def _ffn(h, nw, wg, wu, wd):
    n, d = h.shape
    fp = wg.shape[1]
    tm = min(FFN_TM, n)
    return pl.pallas_call(
        _ffn_body,
        grid=(n // tm, fp // FFN_TF),
        in_specs=[
            pl.BlockSpec((tm, d), lambda i, j: (i, 0)),
            pl.BlockSpec((1, d), lambda i, j: (0, 0)),
            pl.BlockSpec((d, FFN_TF), lambda i, j: (0, j)),
            pl.BlockSpec((d, FFN_TF), lambda i, j: (0, j)),
            pl.BlockSpec((FFN_TF, d), lambda i, j: (j, 0)),
        ],
        out_specs=pl.BlockSpec((tm, d), lambda i, j: (i, 0)),
        out_shape=jax.ShapeDtypeStruct((n, d), F32),
        scratch_shapes=[pltpu.VMEM((tm, d), BF16)],
        compiler_params=_params(("arbitrary", "arbitrary")),
    )(h, nw, wg, wu, wd)


def _norm_proj_body(h_ref, nw_ref, w_ref, o_ref, hn_ref):
    @pl.when(pl.program_id(1) == 0)
    def _():
        hn_ref[...] = _rms(h_ref[...], nw_ref[...]).astype(BF16)

    o_ref[...] = _dot(hn_ref[...], w_ref[...])


def _norm_proj(h, nw, w):
    n, d = h.shape
    c = w.shape[1]
    tm = min(PROJ_TM, n)
    tn = min(PROJ_TN, c)
    return pl.pallas_call(
        _norm_proj_body,
        grid=(n // tm, c // tn),
        in_specs=[
            pl.BlockSpec((tm, d), lambda i, j: (i, 0)),
            pl.BlockSpec((1, d), lambda i, j: (0, 0)),
            pl.BlockSpec((d, tn), lambda i, j: (0, j)),
        ],
        out_specs=pl.BlockSpec((tm, tn), lambda i, j: (i, j)),
        out_shape=jax.ShapeDtypeStruct((n, c), F32),
        scratch_shapes=[pltpu.VMEM((tm, d), BF16)],
        compiler_params=_params(("arbitrary", "arbitrary")),
    )(h, nw, w)


def _out_proj_body(n_in, h_ref, *refs):
    y_refs, w_refs, o_ref = refs[:n_in], refs[n_in:2 * n_in], refs[2 * n_in]
    acc = h_ref[...]
    for y_ref, w_ref in zip(y_refs, w_refs):
        acc = acc + _dot(y_ref[...], w_ref[...])
    o_ref[...] = acc


def _out_proj(h, ys, ws):
    n, d = h.shape
    tm = min(PROJ_TM, n)
    in_specs = [pl.BlockSpec((tm, d), lambda i: (i, 0))]
    in_specs += [pl.BlockSpec((tm, y.shape[1]), lambda i: (i, 0)) for y in ys]
    in_specs += [pl.BlockSpec(w.shape, lambda i: (0, 0)) for w in ws]
    return pl.pallas_call(
        functools.partial(_out_proj_body, len(ys)),
        grid=(n // tm,),
        in_specs=in_specs,
        out_specs=pl.BlockSpec((tm, d), lambda i: (i, 0)),
        out_shape=jax.ShapeDtypeStruct((n, d), F32),
        compiler_params=_params(("arbitrary",)),
    )(h, *ys, *ws)


def _mlstm_body(bias_ref, u_ref, v_ref, og_ref, ig_ref, fg_ref, cw_ref, cb_ref, wq_ref, wk_ref, nw_ref, sk_ref,
                y_ref, ct_ref, n_ref, m_ref, up_ref):
    hh = pl.program_id(1)
    L, dh = u_ref.shape[1], u_ref.shape[2]

    @pl.when(pl.program_id(2) == 0)
    def _():
        ct_ref[...] = jnp.zeros_like(ct_ref)
        n_ref[...] = jnp.zeros_like(n_ref)
        m_ref[...] = jnp.zeros_like(m_ref)
        up_ref[...] = jnp.zeros_like(up_ref)

    u = u_ref[0]
    up = up_ref[...]
    row = lax.broadcasted_iota(jnp.int32, (L, dh), 0)
    cw = cw_ref[...]
    acc = u * cw[ML_CONV - 1:ML_CONV, :] + cb_ref[...]
    for j in range(1, ML_CONV):
        shifted = jnp.where(row < j, pltpu.roll(up, j, 0), pltpu.roll(u, j, 0))
        acc = acc + shifted * cw[ML_CONV - 1 - j:ML_CONV - j, :]
    up_ref[...] = u
    c = acc * jax.nn.sigmoid(acc)
    c16 = c.astype(BF16)
    q = _dot(c16, wq_ref[0])
    k = _dot(c16, wk_ref[0]) * (dh ** -0.5)
    q16, k16, v16 = q.astype(BF16), k.astype(BF16), v_ref[0].astype(BF16)

    f_row = _log_sigmoid(fg_ref[0, 0] + bias_ref[ML_HEADS + hh])
    i_row = ig_ref[0, 0] + bias_ref[hh]
    li = lax.broadcasted_iota(jnp.int32, (L, L), 0)
    si = lax.broadcasted_iota(jnp.int32, (L, L), 1)
    tri = si <= li
    eye = si == li
    f_b = jnp.broadcast_to(f_row, (L, L))
    i_b = jnp.broadcast_to(i_row, (L, L))
    b_col = jnp.sum(jnp.where(tri, f_b, 0.0), axis=1, keepdims=True)
    f_col = jnp.sum(jnp.where(eye, f_b, 0.0), axis=1, keepdims=True)
    i_col = jnp.sum(jnp.where(eye, i_b, 0.0), axis=1, keepdims=True)
    b_row = jnp.sum(jnp.where(li <= si, jnp.broadcast_to(f_col, (L, L)), 0.0), axis=0, keepdims=True)
    a_row = i_row - b_row
    a_col = i_col - b_col
    m_old = m_ref[...]
    mrow = jnp.maximum(m_old, jnp.max(jnp.where(tri, jnp.broadcast_to(a_row, (L, L)), NEG), axis=1, keepdims=True))
    w_intra = jnp.exp(jnp.where(tri, a_row - mrow, NEG))
    w_inter = jnp.exp(m_old - mrow)
    qk = _dot_nt(q16, k16) * w_intra
    num = _dot(qk.astype(BF16), v16) + w_inter * _dot(q16, ct_ref[...].astype(BF16))
    den = jnp.sum(qk, axis=1, keepdims=True) + w_inter * jnp.sum(q * n_ref[...], axis=1, keepdims=True)
    hout = num / jnp.maximum(jnp.abs(den), jnp.exp(-(b_col + mrow)))

    amax = jnp.maximum(m_old, jnp.max(a_row, axis=1, keepdims=True))
    g = jnp.sum(f_row, axis=1, keepdims=True)
    a_prev = jnp.exp(m_old - amax)
    ks = k * jnp.exp(a_col - amax)
    ct_ref[...] = a_prev * ct_ref[...] + _dot_tn(ks.astype(BF16), v16)
    n_ref[...] = a_prev * n_ref[...] + jnp.sum(ks, axis=0, keepdims=True)
    m_ref[...] = g + amax

    hm = _rms(hout, nw_ref[...])
    y_ref[0] = (jax.nn.sigmoid(og_ref[0]) * (hm + sk_ref[...] * c)).astype(BF16)


def _mlstm(proj, gates, gate_bias, conv_w, conv_b, wq, wk, norm_w, skip):
    b, t, _ = proj.shape
    L, dh, nh = ML_CHUNK, ML_HEAD_DIM, ML_HEADS
    vec = lambda: pl.BlockSpec((1, dh), lambda bi, hi, ci: (0, hi))
    return pl.pallas_call(
        _mlstm_body,
        grid=(b, nh, t // L),
        in_specs=[
            pl.BlockSpec(memory_space=pltpu.SMEM),
            pl.BlockSpec((1, L, dh), lambda bi, hi, ci: (bi, ci, hi)),
            pl.BlockSpec((1, L, dh), lambda bi, hi, ci: (bi, ci, nh + hi)),
            pl.BlockSpec((1, L, dh), lambda bi, hi, ci: (bi, ci, 2 * nh + hi)),
            pl.BlockSpec((1, 1, 1, L), lambda bi, hi, ci: (bi, hi, 0, ci)),
            pl.BlockSpec((1, 1, 1, L), lambda bi, hi, ci: (bi, nh + hi, 0, ci)),
            pl.BlockSpec((ML_CONV, dh), lambda bi, hi, ci: (0, hi)),
            vec(),
            pl.BlockSpec((1, dh, dh), lambda bi, hi, ci: (hi, 0, 0)),
            pl.BlockSpec((1, dh, dh), lambda bi, hi, ci: (hi, 0, 0)),
            vec(),
            vec(),
        ],
        out_specs=pl.BlockSpec((1, L, dh), lambda bi, hi, ci: (bi, ci, hi)),
        out_shape=jax.ShapeDtypeStruct((b, t, ML_WIDTH), BF16),
        scratch_shapes=[
            pltpu.VMEM((dh, dh), F32),
            pltpu.VMEM((1, dh), F32),
            pltpu.VMEM((1, 1), F32),
            pltpu.VMEM((L, dh), F32),
        ],
        compiler_params=_params(("arbitrary", "arbitrary", "arbitrary")),
    )(gate_bias, proj, proj, proj, gates, gates, conv_w, conv_b, wq, wk, norm_w, skip)


def _hg_level_tables(L):
    l = np.arange(L)[:, None]
    s = np.arange(L)[None, :]
    mats = [(s <= l)]
    mats.append(s <= (l // 8) * 8 + 3)
    for h in HG_LEVELS:
        mats.append(s <= (l // (2 * h)) * (2 * h) + h - 1)
    stack = np.concatenate(mats, axis=0).astype(np.float32)
    lvl = np.full((L, L), -1.0, np.float32)
    lvl[(l // 8 == s // 8) & (s <= l)] = 0.0
    for idx, h in enumerate(HG_LEVELS):
        sel = (l // (2 * h) == s // (2 * h)) & (l % (2 * h) >= h) & (s % (2 * h) < h)
        lvl[sel] = idx + 1.0
    return stack, lvl


def _hgrn2_body(q_ref, f_ref, i_ref, g_ref, lb_ref, nw_ref, stack_ref, lvl_ref, y_ref, st_ref):
    L = q_ref.shape[1]

    @pl.when(pl.program_id(2) == 0)
    def _():
        st_ref[...] = jnp.zeros_like(st_ref)

    z = f_ref[0]
    k = (1.0 - lb_ref[...]) * jax.nn.sigmoid(-z)
    log_f = jnp.maximum(jnp.log1p(-jnp.minimum(k, HG_MAX_K)), _log_sigmoid(z))
    hq = q_ref[0]
    q = hq * jax.nn.sigmoid(hq)
    v16 = i_ref[0].astype(BF16)

    stack = stack_ref[...]
    hi, mid, lo = _split3(log_f)
    br = _dot(stack, hi) + _dot(stack, mid) + _dot(stack, lo)
    b = br[0:L]
    lvl = lvl_ref[...]
    r0 = br[L:2 * L]
    att = jnp.where(lvl == 0.0, _dot_nt((q * jnp.exp(b - r0)).astype(BF16), (k * jnp.exp(r0 - b)).astype(BF16)), 0.0)
    for idx in range(len(HG_LEVELS)):
        r = br[(idx + 2) * L:(idx + 3) * L]
        qd = (q * jnp.exp(jnp.minimum(b - r, 0.0))).astype(BF16)
        kd = (k * jnp.exp(jnp.minimum(r - b, 0.0))).astype(BF16)
        att = att + jnp.where(lvl == idx + 1.0, _dot_nt(qd, kd), 0.0)

    st = st_ref[...]
    o = _dot(att.astype(BF16), v16) + _dot_nt((q * jnp.exp(b)).astype(BF16), st.astype(BF16))
    g = b[L - 1:L, :]
    st_ref[...] = st * jnp.exp(g) + _dot_tn(v16, (k * jnp.exp(g - b)).astype(BF16))

    hg = g_ref[0]
    y_ref[0] = (_rms(o, nw_ref[...]) * (hg * jax.nn.sigmoid(hg))).astype(BF16)


def _hgrn2(proj, lb, norm_w):
    b, t, _ = proj.shape
    L, dk, nh = HG_CHUNK, HG_KEY_DIM, HG_HEADS
    stack, lvl = _hg_level_tables(L)
    base = 3 * ML_WIDTH // dk
    col = lambda off: pl.BlockSpec((1, L, dk), lambda bi, hi, ci: (bi, ci, base + off * nh + hi))
    vec = lambda: pl.BlockSpec((1, dk), lambda bi, hi, ci: (0, hi))
    return pl.pallas_call(
        _hgrn2_body,
        grid=(b, nh, t // L),
        in_specs=[
            col(0), col(1), col(2), col(3), vec(), vec(),
            pl.BlockSpec(stack.shape, lambda bi, hi, ci: (0, 0)),
            pl.BlockSpec(lvl.shape, lambda bi, hi, ci: (0, 0)),
        ],
        out_specs=pl.BlockSpec((1, L, dk), lambda bi, hi, ci: (bi, ci, hi)),
        out_shape=jax.ShapeDtypeStruct((b, t, HG_WIDTH), BF16),
        scratch_shapes=[pltpu.VMEM((HG_VAL_DIM, dk), F32)],
        compiler_params=_params(("arbitrary", "arbitrary", "arbitrary")),
    )(proj, proj, proj, proj, lb, norm_w, jnp.asarray(stack, BF16), jnp.asarray(lvl))


def _rope_tables(pos):
    half = NSA_HEAD_DIM // 2
    inv_freq = jnp.power(ROPE_THETA, -jnp.arange(half, dtype=F32) / half)
    ang = pos.astype(F32)[:, None] * inv_freq[None, :]
    cos, sin = jnp.cos(ang), jnp.sin(ang)
    return jnp.concatenate([cos, cos], axis=1), jnp.concatenate([-sin, sin], axis=1)


def _rope(x, cos, sin_signed):
    return x * cos + pltpu.roll(x, NSA_HEAD_DIM // 2, 1) * sin_signed


def _head_prep_body(scale, x_ref, nw_ref, cos_ref, sin_ref, o_ref):
    y = _rope(_rms(x_ref[0], nw_ref[...]), cos_ref[...], sin_ref[...])
    o_ref[0, 0] = (y * scale).astype(BF16)


def _head_prep(proj, col0, nheads, nw, cos, sin, scale):
    b, t, _ = proj.shape
    d = NSA_HEAD_DIM
    tt = min(512, t)
    return pl.pallas_call(
        functools.partial(_head_prep_body, scale),
        grid=(b, nheads, t // tt),
        in_specs=[
            pl.BlockSpec((1, tt, d), lambda bi, hi, ti: (bi, ti, col0 + hi)),
            pl.BlockSpec((1, d), lambda bi, hi, ti: (0, 0)),
            pl.BlockSpec((tt, d), lambda bi, hi, ti: (ti, 0)),
            pl.BlockSpec((tt, d), lambda bi, hi, ti: (ti, 0)),
        ],
        out_specs=pl.BlockSpec((1, 1, tt, d), lambda bi, hi, ti: (bi, hi, ti, 0)),
        out_shape=jax.ShapeDtypeStruct((b, nheads, t, d), BF16),
        compiler_params=_params(("arbitrary", "arbitrary", "arbitrary")),
    )(proj, nw, cos, sin)


def _compress_body(do_norm, x_ref, pt_ref, pb_ref, w1t_ref, w1b_ref, b1_ref, w2_ref, nw_ref, cos_ref, sin_ref, o_ref):
    x = x_ref[0, 0]
    nh = x.shape[0]
    top = _dot((x + pt_ref[0]).astype(BF16), w1t_ref[0])
    bot = _dot((x + pb_ref[0]).astype(BF16), w1b_ref[0])
    hid = top + pltpu.roll(bot, nh - 1, 0) + b1_ref[0]
    hid = hid * jax.nn.sigmoid(hid)
    y = _dot(hid.astype(BF16), w2_ref[0])
    if do_norm:
        y = _rope(_rms(y, nw_ref[...]), cos_ref[...], sin_ref[...])
    o_ref[0, 0] = y.astype(BF16)


def _compress(x, which, do_norm, pos, w1, b1, w2, nw, cos, sin):
    b, g, nh, wd = x.shape
    d = NSA_HEAD_DIM
    half = CMP_STRIDE * d
    pos_t = pos[:, :CMP_STRIDE].reshape(2, 1, half)
    pos_b = pos[:, CMP_STRIDE:].reshape(2, 1, half)
    w1t = w1[:, :half].astype(BF16)
    w1b = w1[:, half:].astype(BF16)
    sel = lambda bi, gi: (which, 0, 0)
    return pl.pallas_call(
        functools.partial(_compress_body, do_norm),
        grid=(b, g),
        in_specs=[
            pl.BlockSpec((1, 1, nh, wd), lambda bi, gi: (bi, gi, 0, 0)),
            pl.BlockSpec((1, 1, half), sel),
            pl.BlockSpec((1, 1, half), sel),
            pl.BlockSpec((1, half, CMP_HIDDEN), sel),
            pl.BlockSpec((1, half, CMP_HIDDEN), sel),
            pl.BlockSpec((1, 1, CMP_HIDDEN), sel),
            pl.BlockSpec((1, CMP_HIDDEN, d), sel),
            pl.BlockSpec((1, d), lambda bi, gi: (0, 0)),
            pl.BlockSpec((nh, d), lambda bi, gi: (0, 0)),
            pl.BlockSpec((nh, d), lambda bi, gi: (0, 0)),
        ],
        out_specs=pl.BlockSpec((1, 1, nh, d), lambda bi, gi: (bi, gi, 0, 0)),
        out_shape=jax.ShapeDtypeStruct((b, g, nh, d), BF16),
        compiler_params=_params(("arbitrary", "arbitrary")),
    )(x, pos_t, pos_b, w1t, w1b, b1.reshape(2, 1, CMP_HIDDEN), w2.astype(BF16), nw, cos, sin)


def _nsa_cmp_body(q_ref, kc_ref, vc_ref, cov_ref, oc_ref, pen_ref, imp_ref):
    R, TQ, d = q_ref.shape[2], q_ref.shape[3], q_ref.shape[4]
    ncp = kc_ref.shape[2]
    ns = cov_ref.shape[0]
    t0 = pl.program_id(2) * TQ
    q4 = q_ref[0, 0].reshape(R * TQ, d)
    s = _dot_nt(kc_ref[0, 0], q4)
    n_i = lax.broadcasted_iota(jnp.int32, (ncp, R * TQ), 0)
    t_i = t0 + (lax.broadcasted_iota(jnp.int32, (ncp, R * TQ), 1) & (TQ - 1))
    valid = n_i * CMP_STRIDE + (CMP_BLOCK - 1) <= t_i
    s = jnp.where(valid, s, NEG)
    e = jnp.where(valid, jnp.exp(s - jnp.max(s, axis=0, keepdims=True)), 0.0)
    zsum = jnp.sum(e, axis=0, keepdims=True)
    p = e / jnp.where(zsum > 0.0, zsum, 1.0)
    oc_ref[0, 0] = _dot_tn(p.astype(BF16), vc_ref[0, 0]).reshape(R, TQ, d)

    psum = p[:, 0:TQ]
    for r in range(1, R):
        psum = psum + p[:, r * TQ:(r + 1) * TQ]
    p_hi = psum.astype(BF16)
    p_lo = (psum - p_hi.astype(F32)).astype(BF16)
    cov = cov_ref[...]
    imp = _dot(cov, p_hi) + _dot(cov, p_lo)
    blk = lax.broadcasted_iota(jnp.int32, (ns, TQ), 0)
    tt = t0 + lax.broadcasted_iota(jnp.int32, (ns, TQ), 1)
    cur = lax.shift_right_logical(tt, int(np.log2(SEL_BLOCK)))
    forced = (blk == 0) | (blk == cur) | (blk == cur - 1)
    imp = jnp.where(blk * SEL_BLOCK <= tt, imp + jnp.where(forced, FORCE_BONUS, 0.0), NEG)
    imp_ref[...] = imp

    def count(j, rank):
        x = imp_ref[pl.ds(j, 1), :]
        ge = jnp.where(x >= imp, 1.0, 0.0)
        gt = jnp.where(x > imp, 1.0, 0.0)
        return rank + jnp.where(blk > j, ge, gt)

    rank = lax.fori_loop(0, ns, count, jnp.zeros((ns, TQ), F32))
    pen_t = jnp.where(rank < float(min(SEL_TOPK, ns)), 0.0, NEG)
    pen_ref[0, 0] = pen_t.T.astype(BF16)


def _nsa_cmp(q, k_cmp, v_cmp):
    b, g, r, t, d = q.shape
    ncp = k_cmp.shape[2]
    ns = t // SEL_BLOCK
    tq = NSA_TQ
    ci = np.arange(ncp)[None, :] * CMP_STRIDE
    sj = np.arange(ns)[:, None] * SEL_BLOCK
    cover_t = ((ci < sj + SEL_BLOCK) & (ci + CMP_BLOCK > sj) & (np.arange(ncp)[None, :] < ncp - 1))
    return pl.pallas_call(
        _nsa_cmp_body,
        grid=(b, g, t // tq),
        in_specs=[
            pl.BlockSpec((1, 1, r, tq, d), lambda bi, gi, qi: (bi, gi, 0, qi, 0)),
            pl.BlockSpec((1, 1, ncp, d), lambda bi, gi, qi: (bi, gi, 0, 0)),
            pl.BlockSpec((1, 1, ncp, d), lambda bi, gi, qi: (bi, gi, 0, 0)),
            pl.BlockSpec((ns, ncp), lambda bi, gi, qi: (0, 0)),
        ],
        out_specs=[
            pl.BlockSpec((1, 1, r, tq, d), lambda bi, gi, qi: (bi, gi, 0, qi, 0)),
            pl.BlockSpec((1, 1, tq, ns), lambda bi, gi, qi: (bi, gi, qi, 0)),
        ],
        out_shape=[
            jax.ShapeDtypeStruct((b, g, r, t, d), F32),
            jax.ShapeDtypeStruct((b, g, t, ns), BF16),
        ],
        scratch_shapes=[pltpu.VMEM((ns, tq), F32)],
        compiler_params=_params(("arbitrary", "arbitrary", "arbitrary")),
    )(q, k_cmp, v_cmp, jnp.asarray(cover_t.astype(np.float32), BF16))


def _flash_step(s, v16, m_ref, l_ref, acc_ref):
    m_old = m_ref[...]
    m_new = jnp.maximum(m_old, jnp.max(s, axis=1, keepdims=True))
    alpha = jnp.exp(m_old - m_new)
    p = jnp.exp(s - m_new)
    l_ref[...] = alpha * l_ref[...] + jnp.sum(p, axis=1, keepdims=True)
    acc_ref[...] = alpha * acc_ref[...] + _dot(p.astype(BF16), v16)
    m_ref[...] = m_new


def _nsa_sel_body(q_ref, pen_ref, ks_ref, e_ref, vs_ref, kw_ref, vw_ref, oc_ref, gp_ref, gb_ref, o_ref,
                  m1_ref, l1_ref, a1_ref, m2_ref, l2_ref, a2_ref):
    R, TQ, d = q_ref.shape[2], q_ref.shape[3], q_ref.shape[4]
    TK = NSA_TK
    rows = R * TQ
    qi = pl.program_id(2)
    t0 = qi * TQ
    q4 = q_ref[0, 0].reshape(rows, d)
    pen = pen_ref[0, 0]
    qa = jnp.concatenate([q4, jnp.concatenate([pen] * R, axis=0)], axis=1)
    t_row = t0 + (lax.broadcasted_iota(jnp.int32, (rows, 1), 0) & (TQ - 1))

    for m_ref, l_ref, a_ref in ((m1_ref, l1_ref, a1_ref), (m2_ref, l2_ref, a2_ref)):
        m_ref[...] = jnp.full_like(m_ref, NEG)
        l_ref[...] = jnp.zeros_like(l_ref)
        a_ref[...] = jnp.zeros_like(a_ref)

    def sel_step(kt, carry):
        k0 = pl.multiple_of(kt * TK, TK)
        ka = jnp.concatenate([ks_ref[0, 0, pl.ds(k0, TK), :], e_ref[pl.ds(k0, TK), :]], axis=1)
        s = _dot_nt(qa, ka)
        kpos = k0 + lax.broadcasted_iota(jnp.int32, (rows, TK), 1)
        s = jnp.where(kpos <= t_row, s, NEG)
        _flash_step(s, vs_ref[0, 0, pl.ds(k0, TK), :], m1_ref, l1_ref, a1_ref)
        return carry

    lax.fori_loop(0, (t0 + TQ + TK - 1) // TK, sel_step, 0)

    def win_step(kt, carry):
        k0 = pl.multiple_of(kt * TQ, TQ)
        s = _dot_nt(q4, kw_ref[0, 0, pl.ds(k0, TQ), :])
        kpos = k0 + lax.broadcasted_iota(jnp.int32, (rows, TQ), 1)
        s = jnp.where((kpos <= t_row) & (kpos > t_row - WINDOW), s, NEG)
        _flash_step(s, vw_ref[0, 0, pl.ds(k0, TQ), :], m2_ref, l2_ref, a2_ref)
        return carry

    lax.fori_loop(jnp.maximum(qi - WINDOW // TQ, 0), qi + 1, win_step, 0)

    gate = jax.nn.sigmoid(gp_ref[0, 0] + gb_ref[0])
    o_sel = a1_ref[...] / l1_ref[...]
    o_win = a2_ref[...] / l2_ref[...]
    for r in range(R):
        rs = slice(r * TQ, (r + 1) * TQ)
        o = (gate[:, 3 * r:3 * r + 1] * oc_ref[0, 0, r]
             + gate[:, 3 * r + 1:3 * r + 2] * o_sel[rs]
             + gate[:, 3 * r + 2:3 * r + 3] * o_win[rs])
        o_ref[0, :, r * d:(r + 1) * d] = o.astype(BF16)


def _nsa_sel(q, pen, k_sel, v_sel, k_win, v_win, o_cmp, gate_pre, gate_bias):
    b, g, r, t, d = q.shape
    ns = pen.shape[3]
    tq = NSA_TQ
    rows = r * tq
    onehot = (np.arange(t)[:, None] // SEL_BLOCK == np.arange(ns)[None, :]).astype(np.float32)
    kv = lambda: pl.BlockSpec((1, 1, t, d), lambda bi, gi, qi: (bi, gi, 0, 0))
    qspec = lambda: pl.BlockSpec((1, 1, r, tq, d), lambda bi, gi, qi: (bi, gi, 0, qi, 0))
    gw = gate_pre.shape[3]
    return pl.pallas_call(
        _nsa_sel_body,
        grid=(b, g, t // tq),
        in_specs=[
            qspec(),
            pl.BlockSpec((1, 1, tq, ns), lambda bi, gi, qi: (bi, gi, qi, 0)),
            kv(),
            pl.BlockSpec((t, ns), lambda bi, gi, qi: (0, 0)),
            kv(), kv(), kv(),
            qspec(),
            pl.BlockSpec((1, 1, tq, gw), lambda bi, gi, qi: (bi, gi, qi, 0)),
            pl.BlockSpec((1, 1, gw), lambda bi, gi, qi: (gi, 0, 0)),
        ],
        out_specs=pl.BlockSpec((1, tq, r * d), lambda bi, gi, qi: (bi, qi, gi)),
        out_shape=jax.ShapeDtypeStruct((b, t, g * r * d), BF16),
        scratch_shapes=[
            pltpu.VMEM((rows, 1), F32), pltpu.VMEM((rows, 1), F32), pltpu.VMEM((rows, d), F32),
            pltpu.VMEM((rows, 1), F32), pltpu.VMEM((rows, 1), F32), pltpu.VMEM((rows, d), F32),
        ],
        compiler_params=_params(("arbitrary", "arbitrary", "arbitrary")),
    )(q, pen, k_sel, jnp.asarray(onehot, BF16), v_sel, k_win, v_win, o_cmp, gate_pre, gate_bias)


def _pad_cols(w, mult):
    pad = (-w.shape[1]) % mult
    return jnp.pad(w, ((0, 0), (0, pad))) if pad else w


def _ab_mixer(h, bsz, nw, w_in, w_out, conv_w, conv_b, wq, wk, i_bias, f_bias, ml_norm, ml_skip, lb, hg_norm):
    n, _ = h.shape
    t = n // bsz
    g0 = 3 * ML_WIDTH
    g1 = g0 + 2 * ML_HEADS
    w_main = jnp.concatenate([w_in[:, :g0], w_in[:, g1:]], axis=1).astype(BF16)
    w_gate = _pad_cols(w_in[:, g0:g1], LANES).astype(BF16)
    proj = _norm_proj(h, nw, w_main).reshape(bsz, t, -1)
    gates = _norm_proj(h, nw, w_gate)[:, :2 * ML_HEADS]
    gates = gates.reshape(bsz, t, 2 * ML_HEADS).transpose(0, 2, 1).reshape(bsz, 2 * ML_HEADS, 1, t)
    y_ml = _mlstm(proj, gates, jnp.concatenate([i_bias, f_bias]), conv_w, conv_b.reshape(1, -1),
                  wq.astype(BF16), wk.astype(BF16), ml_norm.reshape(1, -1), ml_skip.reshape(1, -1))
    y_hg = _hgrn2(proj, lb.reshape(1, -1), hg_norm.reshape(1, -1))
    w_out16 = w_out.astype(BF16)
    return _out_proj(h, [y_ml.reshape(n, -1), y_hg.reshape(n, -1)], [w_out16[:ML_WIDTH], w_out16[ML_WIDTH:]])


def _nsa_mixer(h, bsz, nw, w_in, w_out, q_norm, k_norm, cmp_pos, cmp_w1, cmp_b1, cmp_w2, gate_bias):
    n, _ = h.shape
    t = n // bsz
    d, G, R = NSA_HEAD_DIM, NSA_KV_GROUPS, NSA_REP
    c_main = NSA_HEADS * d + 6 * NSA_KV_WIDTH
    proj = _norm_proj(h, nw, w_in[:, :c_main].astype(BF16)).reshape(bsz, t, c_main)
    gp = _norm_proj(h, nw, _pad_cols(w_in[:, c_main:], LANES).astype(BF16))[:, :3 * NSA_HEADS]

    cos, sin = _rope_tables(jnp.arange(t))
    nh = t // CMP_STRIDE
    cos_c, sin_c = _rope_tables(jnp.arange(nh) * CMP_STRIDE + CMP_BLOCK - 1)
    q = _head_prep(proj, 0, NSA_HEADS, q_norm.reshape(1, d), cos, sin, d ** -0.5).reshape(bsz, G, R, t, d)
    kvb = NSA_HEADS
    k_sel = _head_prep(proj, kvb + 2 * G, G, k_norm[1].reshape(1, d), cos, sin, 1.0)
    k_win = _head_prep(proj, kvb + 4 * G, G, k_norm[2].reshape(1, d), cos, sin, 1.0)

    def kv_heads(idx):
        c0 = NSA_HEADS * d + idx * NSA_KV_WIDTH
        return proj[:, :, c0:c0 + NSA_KV_WIDTH].reshape(bsz, t, G, d).transpose(0, 2, 1, 3)

    v_sel = kv_heads(3).astype(BF16)
    v_win = kv_heads(5).astype(BF16)
    k_cmp = _compress(kv_heads(0).reshape(bsz, G, nh, CMP_STRIDE * d), 0, True, cmp_pos, cmp_w1, cmp_b1, cmp_w2,
                      k_norm[0].reshape(1, d), cos_c, sin_c)
    v_cmp = _compress(kv_heads(1).reshape(bsz, G, nh, CMP_STRIDE * d), 1, False, cmp_pos, cmp_w1, cmp_b1, cmp_w2,
                      k_norm[0].reshape(1, d), cos_c, sin_c)
    o_cmp, pen = _nsa_cmp(q, k_cmp, v_cmp)

    gw = 16
    gp = jnp.pad(gp.reshape(bsz, t, G, 3 * R), ((0, 0), (0, 0), (0, 0), (0, gw - 3 * R))).transpose(0, 2, 1, 3)
    gb = jnp.pad(gate_bias.reshape(G, 1, 3 * R), ((0, 0), (0, 0), (0, gw - 3 * R)))
    o = _nsa_sel(q, pen, k_sel, v_sel, k_win, v_win, o_cmp, gp, gb)
    return _out_proj(h, [o.reshape(n, -1)], [w_out.astype(BF16)])


def _ffn_weights(wg, wu, wd):
    pad = D_FF_PAD - D_FF
    wg = jnp.pad(wg, ((0, 0), (0, pad))).astype(BF16)
    wu = jnp.pad(wu, ((0, 0), (0, pad))).astype(BF16)
    wd = jnp.pad(wd, ((0, pad), (0, 0))).astype(BF16)
    return wg, wu, wd


def kernel(x, ffn1_norm, ffn1_w_gate, ffn1_w_up, ffn1_w_down, mix_norm, ffn2_norm, ffn2_w_gate, ffn2_w_up,
           ffn2_w_down, ab_w_in, ab_w_out, ml_conv_w, ml_conv_b, ml_wq, ml_wk, ml_i_bias, ml_f_bias, ml_out_norm,
           ml_skip, hg_lb_logits, hg_out_norm, c_w_in, c_w_out, c_q_norm, c_k_norm, c_cmp_pos, c_cmp_w1, c_cmp_b1,
           c_cmp_w2, c_gate_bias):
    bsz, t, d = x.shape
    depth = ffn1_norm.shape[0]
    lb_soft = jax.nn.softmax(hg_lb_logits.astype(F32), axis=0)
    lb_all = jnp.cumsum(lb_soft, axis=0) - lb_soft[0]
    h = x.reshape(bsz * t, d)
    for layer in range(depth):
        h = _ffn(h, ffn1_norm[layer].reshape(1, d),
                 *_ffn_weights(ffn1_w_gate[layer], ffn1_w_up[layer], ffn1_w_down[layer]))
        j = layer // 2
        nw = mix_norm[layer].reshape(1, d)
        if layer % 2 == 0:
            h = _ab_mixer(h, bsz, nw, ab_w_in[j], ab_w_out[j], ml_conv_w[j], ml_conv_b[j], ml_wq[j], ml_wk[j],
                          ml_i_bias[j], ml_f_bias[j], ml_out_norm[j], ml_skip[j], lb_all[j], hg_out_norm[j])
        else:
            h = _nsa_mixer(h, bsz, nw, c_w_in[j], c_w_out[j], c_q_norm[j], c_k_norm[j], c_cmp_pos[j], c_cmp_w1[j],
                           c_cmp_b1[j], c_cmp_w2[j], c_gate_bias[j])
        h = _ffn(h, ffn2_norm[layer].reshape(1, d),
                 *_ffn_weights(ffn2_w_gate[layer], ffn2_w_up[layer], ffn2_w_down[layer]))
    return h.reshape(bsz, t, d)
```

```python
import functools

import numpy as np
import jax
import jax.numpy as jnp
from jax import lax
from jax.experimental import pallas as pl
from jax.experimental.pallas import tpu as pltpu

F32 = jnp.float32
BF16 = jnp.bfloat16

D_MODEL = 2048
D_FF = 5504
NORM_EPS = 1e-6
ROPE_THETA = 10000.0
NEG = -1e30

ML_HEADS = 4
ML_HEAD_DIM = 256
ML_WIDTH = 1024
ML_CONV = 4

HG_HEADS = 8
HG_KEY_DIM = 128
HG_VAL_DIM = 128
HG_KEY_WIDTH = 1024
HG_WIDTH = 1024
HG_MAX_K = 0.999999

NSA_HEADS = 16
NSA_KV_GROUPS = 4
NSA_REP = NSA_HEADS // NSA_KV_GROUPS
NSA_HEAD_DIM = 128
NSA_KV_WIDTH = NSA_KV_GROUPS * NSA_HEAD_DIM
CMP_BLOCK = 32
CMP_STRIDE = 16
CMP_HIDDEN = 256
SEL_BLOCK = 64
SEL_TOPK = 16
WINDOW = 512
FORCE_BONUS = 1e4

LANES = 128
VMEM_LIMIT = 56 * 1024 * 1024

FFN_TM = 1024
FFN_TF = 256
D_FF_PAD = 5632
PROJ_TM = 512
PROJ_TN = 512
ML_CHUNK = 128
HG_CHUNK = 128
HG_LEVELS = (8, 16, 32, 64)
NSA_CMP_TQ = 128
NSA_TQ = 256


def _params(sem):
    return pltpu.CompilerParams(dimension_semantics=sem, vmem_limit_bytes=VMEM_LIMIT)


def _rms(x, w):
    ms = jnp.mean(x * x, axis=-1, keepdims=True)
    return x * lax.rsqrt(ms + NORM_EPS) * w


def _log_sigmoid(x):
    return jnp.minimum(x, 0.0) - jnp.log1p(jnp.exp(-jnp.abs(x)))


def _dot(a, b):
    return jnp.dot(a, b, preferred_element_type=F32)


def _dot_nt(a, b):
    return lax.dot_general(a, b, (((1,), (1,)), ((), ())), preferred_element_type=F32)


def _dot_tn(a, b):
    return lax.dot_general(a, b, (((0,), (0,)), ((), ())), preferred_element_type=F32)


def _split3(x):
    hi = x.astype(BF16)
    r1 = x - hi.astype(F32)
    mid = r1.astype(BF16)
    lo = (r1 - mid.astype(F32)).astype(BF16)
    return hi, mid, lo


def _ffn_body(h_ref, nw_ref, wg_ref, wu_ref, wd_ref, o_ref, hn_ref):
    @pl.when(pl.program_id(1) == 0)
    def _():
        x = h_ref[...]
        hn_ref[...] = _rms(x, nw_ref[...]).astype(BF16)
        o_ref[...] = x

    hn = hn_ref[...]
    g = _dot(hn, wg_ref[...])
    u = _dot(hn, wu_ref[...])
    a = (g * jax.nn.sigmoid(g) * u).astype(BF16)
    o_ref[...] += 0.5 * _dot(a, wd_ref[...])


def _ffn(h, nw, wg, wu, wd):
    n, d = h.shape
    fp = wg.shape[1]
    tm = min(FFN_TM, n)
    return pl.pallas_call(
        _ffn_body,
        grid=(n // tm, fp // FFN_TF),
        in_specs=[
            pl.BlockSpec((tm, d), lambda i, j: (i, 0)),
            pl.BlockSpec((1, d), lambda i, j: (0, 0)),
            pl.BlockSpec((d, FFN_TF), lambda i, j: (0, j)),
            pl.BlockSpec((d, FFN_TF), lambda i, j: (0, j)),
            pl.BlockSpec((FFN_TF, d), lambda i, j: (j, 0)),
        ],
        out_specs=pl.BlockSpec((tm, d), lambda i, j: (i, 0)),
        out_shape=jax.ShapeDtypeStruct((n, d), F32),
        scratch_shapes=[pltpu.VMEM((tm, d), BF16)],
        compiler_params=_params(("arbitrary", "arbitrary")),
    )(h, nw, wg, wu, wd)


def _norm_proj_body(h_ref, nw_ref, w_ref, o_ref, hn_ref):
    @pl.when(pl.program_id(1) == 0)
    def _():
        hn_ref[...] = _rms(h_ref[...], nw_ref[...]).astype(BF16)

    o_ref[...] = _dot(hn_ref[...], w_ref[...])


def _norm_proj(h, nw, w):
    n, d = h.shape
    c = w.shape[1]
    tm = min(PROJ_TM, n)
    tn = min(PROJ_TN, c)
    return pl.pallas_call(
        _norm_proj_body,
        grid=(n // tm, c // tn),
        in_specs=[
            pl.BlockSpec((tm, d), lambda i, j: (i, 0)),
            pl.BlockSpec((1, d), lambda i, j: (0, 0)),
            pl.BlockSpec((d, tn), lambda i, j: (0, j)),
        ],
        out_specs=pl.BlockSpec((tm, tn), lambda i, j: (i, j)),
        out_shape=jax.ShapeDtypeStruct((n, c), F32),
        scratch_shapes=[pltpu.VMEM((tm, d), BF16)],
        compiler_params=_params(("arbitrary", "arbitrary")),
    )(h, nw, w)


def _out_proj_body(n_in, h_ref, *refs):
    y_refs, w_refs, o_ref = refs[:n_in], refs[n_in:2 * n_in], refs[2 * n_in]
    acc = h_ref[...]
    for y_ref, w_ref in zip(y_refs, w_refs):
        acc = acc + _dot(y_ref[...], w_ref[...])
    o_ref[...] = acc


def _out_proj(h, ys, ws):
    n, d = h.shape
    tm = min(PROJ_TM, n)
    in_specs = [pl.BlockSpec((tm, d), lambda i: (i, 0))]
    in_specs += [pl.BlockSpec((tm, y.shape[1]), lambda i: (i, 0)) for y in ys]
    in_specs += [pl.BlockSpec(w.shape, lambda i: (0, 0)) for w in ws]
    return pl.pallas_call(
        functools.partial(_out_proj_body, len(ys)),
        grid=(n // tm,),
        in_specs=in_specs,
        out_specs=pl.BlockSpec((tm, d), lambda i: (i, 0)),
        out_shape=jax.ShapeDtypeStruct((n, d), F32),
        compiler_params=_params(("arbitrary",)),
    )(h, *ys, *ws)


def _mlstm_body(bias_ref, u_ref, v_ref, og_ref, ig_ref, fg_ref, cw_ref, cb_ref, wq_ref, wk_ref, nw_ref, sk_ref,
                y_ref, ct_ref, n_ref, m_ref, up_ref):
    hh = pl.program_id(1)
    L, dh = u_ref.shape[1], u_ref.shape[2]

    @pl.when(pl.program_id(2) == 0)
    def _():
        ct_ref[...] = jnp.zeros_like(ct_ref)
        n_ref[...] = jnp.zeros_like(n_ref)
        m_ref[...] = jnp.zeros_like(m_ref)
        up_ref[...] = jnp.zeros_like(up_ref)

    u = u_ref[0]
    up = up_ref[...]
    row = lax.broadcasted_iota(jnp.int32, (L, dh), 0)
    cw = cw_ref[...]
    acc = u * cw[ML_CONV - 1:ML_CONV, :] + cb_ref[...]
    for j in range(1, ML_CONV):
        shifted = jnp.where(row < j, pltpu.roll(up, j, 0), pltpu.roll(u, j, 0))
        acc = acc + shifted * cw[ML_CONV - 1 - j:ML_CONV - j, :]
    up_ref[...] = u
    c = acc * jax.nn.sigmoid(acc)
    c16 = c.astype(BF16)
    q = _dot(c16, wq_ref[0])
    k = _dot(c16, wk_ref[0]) * (dh ** -0.5)
    q16, k16, v16 = q.astype(BF16), k.astype(BF16), v_ref[0].astype(BF16)

    f_row = _log_sigmoid(fg_ref[0, 0] + bias_ref[ML_HEADS + hh])
    i_row = ig_ref[0, 0] + bias_ref[hh]
    li = lax.broadcasted_iota(jnp.int32, (L, L), 0)
    si = lax.broadcasted_iota(jnp.int32, (L, L), 1)
    tri = si <= li
    eye = si == li
    f_b = jnp.broadcast_to(f_row, (L, L))
    i_b = jnp.broadcast_to(i_row, (L, L))
    b_col = jnp.sum(jnp.where(tri, f_b, 0.0), axis=1, keepdims=True)
    f_col = jnp.sum(jnp.where(eye, f_b, 0.0), axis=1, keepdims=True)
    i_col = jnp.sum(jnp.where(eye, i_b, 0.0), axis=1, keepdims=True)
    b_row = jnp.sum(jnp.where(li <= si, jnp.broadcast_to(f_col, (L, L)), 0.0), axis=0, keepdims=True)
    a_row = i_row - b_row
    a_col = i_col - b_col
    m_old = m_ref[...]
    mrow = jnp.maximum(m_old, jnp.max(jnp.where(tri, jnp.broadcast_to(a_row, (L, L)), NEG), axis=1, keepdims=True))
    w_intra = jnp.exp(jnp.where(tri, a_row - mrow, NEG))
    w_inter = jnp.exp(m_old - mrow)
    qk = _dot_nt(q16, k16) * w_intra
    num = _dot(qk.astype(BF16), v16) + w_inter * _dot(q16, ct_ref[...].astype(BF16))
    den = jnp.sum(qk, axis=1, keepdims=True) + w_inter * jnp.sum(q * n_ref[...], axis=1, keepdims=True)
    hout = num / jnp.maximum(jnp.abs(den), jnp.exp(-(b_col + mrow)))

    amax = jnp.maximum(m_old, jnp.max(a_row, axis=1, keepdims=True))
    g = jnp.sum(f_row, axis=1, keepdims=True)
    a_prev = jnp.exp(m_old - amax)
    ks = k * jnp.exp(a_col - amax)
    ct_ref[...] = a_prev * ct_ref[...] + _dot_tn(ks.astype(BF16), v16)
    n_ref[...] = a_prev * n_ref[...] + jnp.sum(ks, axis=0, keepdims=True)
    m_ref[...] = g + amax

    hm = _rms(hout, nw_ref[...])
    y_ref[0] = (jax.nn.sigmoid(og_ref[0]) * (hm + sk_ref[...] * c)).astype(BF16)


def _mlstm(proj, gates, gate_bias, conv_w, conv_b, wq, wk, norm_w, skip):
    b, t, _ = proj.shape
    L, dh, nh = ML_CHUNK, ML_HEAD_DIM, ML_HEADS
    vec = lambda: pl.BlockSpec((1, dh), lambda bi, hi, ci: (0, hi))
    return pl.pallas_call(
        _mlstm_body,
        grid=(b, nh, t // L),
        in_specs=[
            pl.BlockSpec(memory_space=pltpu.SMEM),
            pl.BlockSpec((1, L, dh), lambda bi, hi, ci: (bi, ci, hi)),
            pl.BlockSpec((1, L, dh), lambda bi, hi, ci: (bi, ci, nh + hi)),
            pl.BlockSpec((1, L, dh), lambda bi, hi, ci: (bi, ci, 2 * nh + hi)),
            pl.BlockSpec((1, 1, 1, L), lambda bi, hi, ci: (bi, hi, 0, ci)),
            pl.BlockSpec((1, 1, 1, L), lambda bi, hi, ci: (bi, nh + hi, 0, ci)),
            pl.BlockSpec((ML_CONV, dh), lambda bi, hi, ci: (0, hi)),
            vec(),
            pl.BlockSpec((1, dh, dh), lambda bi, hi, ci: (hi, 0, 0)),
            pl.BlockSpec((1, dh, dh), lambda bi, hi, ci: (hi, 0, 0)),
            vec(),
            vec(),
        ],
        out_specs=pl.BlockSpec((1, L, dh), lambda bi, hi, ci: (bi, ci, hi)),
        out_shape=jax.ShapeDtypeStruct((b, t, ML_WIDTH), BF16),
        scratch_shapes=[
            pltpu.VMEM((dh, dh), F32),
            pltpu.VMEM((1, dh), F32),
            pltpu.VMEM((1, 1), F32),
            pltpu.VMEM((L, dh), F32),
        ],
        compiler_params=_params(("arbitrary", "arbitrary", "arbitrary")),
    )(gate_bias, proj, proj, proj, gates, gates, conv_w, conv_b, wq, wk, norm_w, skip)


def _hg_level_tables(L):
    l = np.arange(L)[:, None]
    s = np.arange(L)[None, :]
    mats = [(s <= l)]
    mats.append(s <= (l // 8) * 8 + 3)
    for h in HG_LEVELS:
        mats.append(s <= (l // (2 * h)) * (2 * h) + h - 1)
    stack = np.concatenate(mats, axis=0).astype(np.float32)
    lvl = np.full((L, L), -1.0, np.float32)
    lvl[(l // 8 == s // 8) & (s <= l)] = 0.0
    for idx, h in enumerate(HG_LEVELS):
        sel = (l // (2 * h) == s // (2 * h)) & (l % (2 * h) >= h) & (s % (2 * h) < h)
        lvl[sel] = idx + 1.0
    return stack, lvl


def _hgrn2_body(q_ref, f_ref, i_ref, g_ref, lb_ref, nw_ref, stack_ref, lvl_ref, y_ref, st_ref):
    L = q_ref.shape[1]

    @pl.when(pl.program_id(2) == 0)
    def _():
        st_ref[...] = jnp.zeros_like(st_ref)

    z = f_ref[0]
    k = (1.0 - lb_ref[...]) * jax.nn.sigmoid(-z)
    log_f = jnp.maximum(jnp.log1p(-jnp.minimum(k, HG_MAX_K)), _log_sigmoid(z))
    hq = q_ref[0]
    q = hq * jax.nn.sigmoid(hq)
    v16 = i_ref[0].astype(BF16)

    stack = stack_ref[...]
    hi, mid, lo = _split3(log_f)
    br = _dot(stack, hi) + _dot(stack, mid) + _dot(stack, lo)
    b = br[0:L]
    lvl = lvl_ref[...]
    r0 = br[L:2 * L]
    att = jnp.where(lvl == 0.0, _dot_nt((q * jnp.exp(b - r0)).astype(BF16), (k * jnp.exp(r0 - b)).astype(BF16)), 0.0)
    for idx in range(len(HG_LEVELS)):
        r = br[(idx + 2) * L:(idx + 3) * L]
        qd = (q * jnp.exp(jnp.minimum(b - r, 0.0))).astype(BF16)
        kd = (k * jnp.exp(jnp.minimum(r - b, 0.0))).astype(BF16)
        att = att + jnp.where(lvl == idx + 1.0, _dot_nt(qd, kd), 0.0)

    st = st_ref[...]
    o = _dot(att.astype(BF16), v16) + _dot_nt((q * jnp.exp(b)).astype(BF16), st.astype(BF16))
    g = b[L - 1:L, :]
    st_ref[...] = st * jnp.exp(g) + _dot_tn(v16, (k * jnp.exp(g - b)).astype(BF16))

    hg = g_ref[0]
    y_ref[0] = (_rms(o, nw_ref[...]) * (hg * jax.nn.sigmoid(hg))).astype(BF16)


def _hgrn2(proj, lb, norm_w):
    b, t, _ = proj.shape
    L, dk, nh = HG_CHUNK, HG_KEY_DIM, HG_HEADS
    stack, lvl = _hg_level_tables(L)
    base = 3 * ML_WIDTH // dk
    col = lambda off: pl.BlockSpec((1, L, dk), lambda bi, hi, ci: (bi, ci, base + off * nh + hi))
    vec = lambda: pl.BlockSpec((1, dk), lambda bi, hi, ci: (0, hi))
    return pl.pallas_call(
        _hgrn2_body,
        grid=(b, nh, t // L),
        in_specs=[
            col(0), col(1), col(2), col(3), vec(), vec(),
            pl.BlockSpec(stack.shape, lambda bi, hi, ci: (0, 0)),
            pl.BlockSpec(lvl.shape, lambda bi, hi, ci: (0, 0)),
        ],
        out_specs=pl.BlockSpec((1, L, dk), lambda bi, hi, ci: (bi, ci, hi)),
        out_shape=jax.ShapeDtypeStruct((b, t, HG_WIDTH), BF16),
        scratch_shapes=[pltpu.VMEM((HG_VAL_DIM, dk), F32)],
        compiler_params=_params(("arbitrary", "arbitrary", "arbitrary")),
    )(proj, proj, proj, proj, lb, norm_w, jnp.asarray(stack, BF16), jnp.asarray(lvl))


def _rope_tables(pos):
    half = NSA_HEAD_DIM // 2
    inv_freq = jnp.power(ROPE_THETA, -jnp.arange(half, dtype=F32) / half)
    ang = pos.astype(F32)[:, None] * inv_freq[None, :]
    cos, sin = jnp.cos(ang), jnp.sin(ang)
    return jnp.concatenate([cos, cos], axis=1), jnp.concatenate([-sin, sin], axis=1)


def _rope(x, cos, sin_signed):
    return x * cos + pltpu.roll(x, NSA_HEAD_DIM // 2, 1) * sin_signed


def _head_prep_body(scale, x_ref, nw_ref, cos_ref, sin_ref, o_ref):
    y = _rope(_rms(x_ref[0], nw_ref[...]), cos_ref[...], sin_ref[...])
    o_ref[0, 0] = (y * scale).astype(BF16)


def _head_prep(proj, col0, nheads, nw, cos, sin, scale):
    b, t, _ = proj.shape
    d = NSA_HEAD_DIM
    tt = min(512, t)
    return pl.pallas_call(
        functools.partial(_head_prep_body, scale),
        grid=(b, nheads, t // tt),
        in_specs=[
            pl.BlockSpec((1, tt, d), lambda bi, hi, ti: (bi, ti, col0 + hi)),
            pl.BlockSpec((1, d), lambda bi, hi, ti: (0, 0)),
            pl.BlockSpec((tt, d), lambda bi, hi, ti: (ti, 0)),
            pl.BlockSpec((tt, d), lambda bi, hi, ti: (ti, 0)),
        ],
        out_specs=pl.BlockSpec((1, 1, tt, d), lambda bi, hi, ti: (bi, hi, ti, 0)),
        out_shape=jax.ShapeDtypeStruct((b, nheads, t, d), BF16),
        compiler_params=_params(("arbitrary", "arbitrary", "arbitrary")),
    )(proj, nw, cos, sin)


def _compress_body(do_norm, x_ref, pt_ref, pb_ref, w1t_ref, w1b_ref, b1_ref, w2_ref, nw_ref, cos_ref, sin_ref, o_ref):
    x = x_ref[0, 0]
    nh = x.shape[0]
    top = _dot((x + pt_ref[0]).astype(BF16), w1t_ref[0])
    bot = _dot((x + pb_ref[0]).astype(BF16), w1b_ref[0])
    hid = top + pltpu.roll(bot, nh - 1, 0) + b1_ref[0]
    hid = hid * jax.nn.sigmoid(hid)
    y = _dot(hid.astype(BF16), w2_ref[0])
    if do_norm:
        y = _rope(_rms(y, nw_ref[...]), cos_ref[...], sin_ref[...])
    o_ref[0, 0] = y.astype(BF16)


def _compress(x, which, do_norm, pos, w1, b1, w2, nw, cos, sin):
    b, g, nh, wd = x.shape
    d = NSA_HEAD_DIM
    half = CMP_STRIDE * d
    pos_t = pos[:, :CMP_STRIDE].reshape(2, 1, half)
    pos_b = pos[:, CMP_STRIDE:].reshape(2, 1, half)
    w1t = w1[:, :half].astype(BF16)
    w1b = w1[:, half:].astype(BF16)
    sel = lambda bi, gi: (which, 0, 0)
    return pl.pallas_call(
        functools.partial(_compress_body, do_norm),
        grid=(b, g),
        in_specs=[
            pl.BlockSpec((1, 1, nh, wd), lambda bi, gi: (bi, gi, 0, 0)),
            pl.BlockSpec((1, 1, half), sel),
            pl.BlockSpec((1, 1, half), sel),
            pl.BlockSpec((1, half, CMP_HIDDEN), sel),
            pl.BlockSpec((1, half, CMP_HIDDEN), sel),
            pl.BlockSpec((1, 1, CMP_HIDDEN), sel),
            pl.BlockSpec((1, CMP_HIDDEN, d), sel),
            pl.BlockSpec((1, d), lambda bi, gi: (0, 0)),
            pl.BlockSpec((nh, d), lambda bi, gi: (0, 0)),
            pl.BlockSpec((nh, d), lambda bi, gi: (0, 0)),
        ],
        out_specs=pl.BlockSpec((1, 1, nh, d), lambda bi, gi: (bi, gi, 0, 0)),
        out_shape=jax.ShapeDtypeStruct((b, g, nh, d), BF16),
        compiler_params=_params(("arbitrary", "arbitrary")),
    )(x, pos_t, pos_b, w1t, w1b, b1.reshape(2, 1, CMP_HIDDEN), w2.astype(BF16), nw, cos, sin)


def _nsa_cmp_body(q_ref, kc_ref, vc_ref, cov_ref, oc_ref, pen_ref, imp_ref):
    R, TQ, d = q_ref.shape[2], q_ref.shape[3], q_ref.shape[4]
    ncp = kc_ref.shape[2]
    ns = cov_ref.shape[0]
    t0 = pl.program_id(2) * TQ
    q4 = q_ref[0, 0].reshape(R * TQ, d)
    s = _dot_nt(kc_ref[0, 0], q4)
    n_i = lax.broadcasted_iota(jnp.int32, (ncp, R * TQ), 0)
    t_i = t0 + (lax.broadcasted_iota(jnp.int32, (ncp, R * TQ), 1) & (TQ - 1))
    valid = n_i * CMP_STRIDE + (CMP_BLOCK - 1) <= t_i
    s = jnp.where(valid, s, NEG)
    e = jnp.where(valid, jnp.exp(s - jnp.max(s, axis=0, keepdims=True)), 0.0)
    zsum = jnp.sum(e, axis=0, keepdims=True)
    p = e / jnp.where(zsum > 0.0, zsum, 1.0)
    oc_ref[0, 0] = _dot_tn(p.astype(BF16), vc_ref[0, 0]).reshape(R, TQ, d)

    psum = p[:, 0:TQ]
    for r in range(1, R):
        psum = psum + p[:, r * TQ:(r + 1) * TQ]
    p_hi = psum.astype(BF16)
    p_lo = (psum - p_hi.astype(F32)).astype(BF16)
    cov = cov_ref[...]
    imp = _dot(cov, p_hi) + _dot(cov, p_lo)
    blk = lax.broadcasted_iota(jnp.int32, (ns, TQ), 0)
    tt = t0 + lax.broadcasted_iota(jnp.int32, (ns, TQ), 1)
    cur = lax.shift_right_logical(tt, int(np.log2(SEL_BLOCK)))
    forced = (blk == 0) | (blk == cur) | (blk == cur - 1)
    imp = jnp.where(blk * SEL_BLOCK <= tt, imp + jnp.where(forced, FORCE_BONUS, 0.0), NEG)
    imp_ref[...] = imp

    def count(j, rank):
        x = imp_ref[pl.ds(j, 1), :]
        ge = jnp.where(x >= imp, 1.0, 0.0)
        gt = jnp.where(x > imp, 1.0, 0.0)
        return rank + jnp.where(blk > j, ge, gt)

    n_causal = jnp.minimum((t0 + TQ + SEL_BLOCK - 1) // SEL_BLOCK, ns)
    rank = lax.fori_loop(0, n_causal, count, jnp.zeros((ns, TQ), F32))
    pen_t = jnp.where(rank < float(min(SEL_TOPK, ns)), 0.0, NEG)
    pen_ref[0, 0] = pen_t.T.astype(BF16)


def _nsa_cmp(q, k_cmp, v_cmp):
    b, g, r, t, d = q.shape
    ncp = k_cmp.shape[2]
    ns = t // SEL_BLOCK
    tq = min(NSA_CMP_TQ, t)
    ci = np.arange(ncp)[None, :] * CMP_STRIDE
    sj = np.arange(ns)[:, None] * SEL_BLOCK
    cover_t = ((ci < sj + SEL_BLOCK) & (ci + CMP_BLOCK > sj) & (np.arange(ncp)[None, :] < ncp - 1))
    return pl.pallas_call(
        _nsa_cmp_body,
        grid=(b, g, t // tq),
        in_specs=[
            pl.BlockSpec((1, 1, r, tq, d), lambda bi, gi, qi: (bi, gi, 0, qi, 0)),
            pl.BlockSpec((1, 1, ncp, d), lambda bi, gi, qi: (bi, gi, 0, 0)),
            pl.BlockSpec((1, 1, ncp, d), lambda bi, gi, qi: (bi, gi, 0, 0)),
            pl.BlockSpec((ns, ncp), lambda bi, gi, qi: (0, 0)),
        ],
        out_specs=[
            pl.BlockSpec((1, 1, r, tq, d), lambda bi, gi, qi: (bi, gi, 0, qi, 0)),
            pl.BlockSpec((1, 1, tq, ns), lambda bi, gi, qi: (bi, gi, qi, 0)),
        ],
        out_shape=[
            jax.ShapeDtypeStruct((b, g, r, t, d), F32),
            jax.ShapeDtypeStruct((b, g, t, ns), BF16),
        ],
        scratch_shapes=[pltpu.VMEM((ns, tq), F32)],
        compiler_params=_params(("arbitrary", "arbitrary", "arbitrary")),
    )(q, k_cmp, v_cmp, jnp.asarray(cover_t.astype(np.float32), BF16))


def _flash_step(s, v16, m_ref, l_ref, acc_ref):
    m_prev = m_ref[...]
    m_next = jnp.maximum(m_prev, jnp.max(functools.reduce(jnp.maximum, s), axis=1, keepdims=True))
    alpha = jnp.exp(m_prev - m_next)
    p = [jnp.exp(sc - m_next) for sc in s]
    l_ref[...] = alpha * l_ref[...] + jnp.sum(functools.reduce(jnp.add, p), axis=1, keepdims=True)
    p16 = jnp.concatenate([pc.astype(BF16) for pc in p], axis=1)
    acc_ref[...] = alpha * acc_ref[...] + _dot(p16, v16)
    m_ref[...] = m_next


def _col_groups(s):
    return [s[:, c * LANES:(c + 1) * LANES] for c in range(s.shape[1] // LANES)]


def _nsa_sel_body(q_ref, pen_ref, ks_ref, e_ref, vs_ref, kw_ref, vw_ref, oc_ref, gp_ref, gb_ref, o_ref,
                  m1_ref, l1_ref, a1_ref, m2_ref, l2_ref, a2_ref):
    R, TQ, d = q_ref.shape[2], q_ref.shape[3], q_ref.shape[4]
    rows = R * TQ
    qi = pl.program_id(2)
    q4 = q_ref[0, 0].reshape(rows, d)
    pen = pen_ref[0, 0]
    qa = jnp.concatenate([q4, jnp.concatenate([pen] * R, axis=0)], axis=1)
    t_rel = lax.broadcasted_iota(jnp.int32, (rows, LANES), 0) & (TQ - 1)
    lane = lax.broadcasted_iota(jnp.int32, (rows, LANES), 1)

    for m_ref, l_ref, a_ref in ((m1_ref, l1_ref, a1_ref), (m2_ref, l2_ref, a2_ref)):
        m_ref[...] = jnp.full_like(m_ref, NEG)
        l_ref[...] = jnp.zeros_like(l_ref)
        a_ref[...] = jnp.zeros_like(a_ref)

    def sel_scores(kt):
        k0 = pl.multiple_of(kt * TQ, TQ)
        ka = jnp.concatenate([ks_ref[0, 0, pl.ds(k0, TQ), :], e_ref[pl.ds(k0, TQ), :]], axis=1)
        return _col_groups(_dot_nt(qa, ka)), vs_ref[0, 0, pl.ds(k0, TQ), :]

    def win_scores(kt):
        k0 = pl.multiple_of(kt * TQ, TQ)
        return _col_groups(_dot_nt(q4, kw_ref[0, 0, pl.ds(k0, TQ), :])), vw_ref[0, 0, pl.ds(k0, TQ), :]

    def causal(s):
        return [jnp.where(c * LANES + lane <= t_rel, sc, NEG) for c, sc in enumerate(s)]

    s, v = sel_scores(qi)
    _flash_step(causal(s), v, m1_ref, l1_ref, a1_ref)

    def sel_step(kt, carry):
        s, v = sel_scores(kt)
        _flash_step(s, v, m1_ref, l1_ref, a1_ref)
        return carry

    lax.fori_loop(0, qi, sel_step, 0)

    s, v = win_scores(qi)
    _flash_step(causal(s), v, m2_ref, l2_ref, a2_ref)
    n_back = WINDOW // TQ

    def win_step(kt, carry):
        s, v = win_scores(kt)
        _flash_step(s, v, m2_ref, l2_ref, a2_ref)
        return carry

    lax.fori_loop(jnp.maximum(qi - n_back + 1, 0), qi, win_step, 0)

    @pl.when(qi >= n_back)
    def _():
        s, v = win_scores(qi - n_back)
        s = [jnp.where(c * LANES + lane > t_rel, sc, NEG) for c, sc in enumerate(s)]
        _flash_step(s, v, m2_ref, l2_ref, a2_ref)

    gate = jax.nn.sigmoid(gp_ref[0, 0] + gb_ref[0])
    o_sel = a1_ref[...] / l1_ref[...]
    o_win = a2_ref[...] / l2_ref[...]
    for r in range(R):
        rs = slice(r * TQ, (r + 1) * TQ)
        o = (gate[:, 3 * r:3 * r + 1] * oc_ref[0, 0, r]
             + gate[:, 3 * r + 1:3 * r + 2] * o_sel[rs]
             + gate[:, 3 * r + 2:3 * r + 3] * o_win[rs])
        o_ref[0, :, r * d:(r + 1) * d] = o.astype(BF16)


def _nsa_sel(q, pen, k_sel, v_sel, k_win, v_win, o_cmp, gate_pre, gate_bias):
    b, g, r, t, d = q.shape
    ns = pen.shape[3]
    tq = min(NSA_TQ, t)
    assert WINDOW % tq == 0 and t % tq == 0 and d == LANES
    rows = r * tq
    onehot = (np.arange(t)[:, None] // SEL_BLOCK == np.arange(ns)[None, :]).astype(np.float32)
    kv = lambda: pl.BlockSpec((1, 1, t, d), lambda bi, gi, qi: (bi, gi, 0, 0))
    qspec = lambda: pl.BlockSpec((1, 1, r, tq, d), lambda bi, gi, qi: (bi, gi, 0, qi, 0))
    gw = gate_pre.shape[3]
    stat = lambda: pltpu.VMEM((rows, LANES), F32)
    return pl.pallas_call(
        _nsa_sel_body,
        grid=(b, g, t // tq),
        in_specs=[
            qspec(),
            pl.BlockSpec((1, 1, tq, ns), lambda bi, gi, qi: (bi, gi, qi, 0)),
            kv(),
            pl.BlockSpec((t, ns), lambda bi, gi, qi: (0, 0)),
            kv(), kv(), kv(),
            qspec(),
            pl.BlockSpec((1, 1, tq, gw), lambda bi, gi, qi: (bi, gi, qi, 0)),
            pl.BlockSpec((1, 1, gw), lambda bi, gi, qi: (gi, 0, 0)),
        ],
        out_specs=pl.BlockSpec((1, tq, r * d), lambda bi, gi, qi: (bi, qi, gi)),
        out_shape=jax.ShapeDtypeStruct((b, t, g * r * d), BF16),
        scratch_shapes=[stat(), stat(), stat(), stat(), stat(), stat()],
        compiler_params=_params(("arbitrary", "arbitrary", "arbitrary")),
    )(q, pen, k_sel, jnp.asarray(onehot, BF16), v_sel, k_win, v_win, o_cmp, gate_pre, gate_bias)


def _pad_cols(w, mult):
    pad = (-w.shape[1]) % mult
    return jnp.pad(w, ((0, 0), (0, pad))) if pad else w


def _ab_mixer(h, bsz, nw, w_in, w_out, conv_w, conv_b, wq, wk, i_bias, f_bias, ml_norm, ml_skip, lb, hg_norm):
    n, _ = h.shape
    t = n // bsz
    g0 = 3 * ML_WIDTH
    g1 = g0 + 2 * ML_HEADS
    w_main = jnp.concatenate([w_in[:, :g0], w_in[:, g1:]], axis=1).astype(BF16)
    w_gate = _pad_cols(w_in[:, g0:g1], LANES).astype(BF16)
    proj = _norm_proj(h, nw, w_main).reshape(bsz, t, -1)
    gates = _norm_proj(h, nw, w_gate)[:, :2 * ML_HEADS]
    gates = gates.reshape(bsz, t, 2 * ML_HEADS).transpose(0, 2, 1).reshape(bsz, 2 * ML_HEADS, 1, t)
    y_ml = _mlstm(proj, gates, jnp.concatenate([i_bias, f_bias]), conv_w, conv_b.reshape(1, -1),
                  wq.astype(BF16), wk.astype(BF16), ml_norm.reshape(1, -1), ml_skip.reshape(1, -1))
    y_hg = _hgrn2(proj, lb.reshape(1, -1), hg_norm.reshape(1, -1))
    w_out16 = w_out.astype(BF16)
    return _out_proj(h, [y_ml.reshape(n, -1), y_hg.reshape(n, -1)], [w_out16[:ML_WIDTH], w_out16[ML_WIDTH:]])


def _nsa_mixer(h, bsz, nw, w_in, w_out, q_norm, k_norm, cmp_pos, cmp_w1, cmp_b1, cmp_w2, gate_bias):
    n, _ = h.shape
    t = n // bsz
    d, G, R = NSA_HEAD_DIM, NSA_KV_GROUPS, NSA_REP
    c_main = NSA_HEADS * d + 6 * NSA_KV_WIDTH
    proj = _norm_proj(h, nw, w_in[:, :c_main].astype(BF16)).reshape(bsz, t, c_main)
    gp = _norm_proj(h, nw, _pad_cols(w_in[:, c_main:], LANES).astype(BF16))[:, :3 * NSA_HEADS]

    cos, sin = _rope_tables(jnp.arange(t))
    nh = t // CMP_STRIDE
    cos_c, sin_c = _rope_tables(jnp.arange(nh) * CMP_STRIDE + CMP_BLOCK - 1)
    q = _head_prep(proj, 0, NSA_HEADS, q_norm.reshape(1, d), cos, sin, d ** -0.5).reshape(bsz, G, R, t, d)
    kvb = NSA_HEADS
    k_sel = _head_prep(proj, kvb + 2 * G, G, k_norm[1].reshape(1, d), cos, sin, 1.0)
    k_win = _head_prep(proj, kvb + 4 * G, G, k_norm[2].reshape(1, d), cos, sin, 1.0)

    def kv_heads(idx):
        c0 = NSA_HEADS * d + idx * NSA_KV_WIDTH
        return proj[:, :, c0:c0 + NSA_KV_WIDTH].reshape(bsz, t, G, d).transpose(0, 2, 1, 3)

    v_sel = kv_heads(3).astype(BF16)
    v_win = kv_heads(5).astype(BF16)
    k_cmp = _compress(kv_heads(0).reshape(bsz, G, nh, CMP_STRIDE * d), 0, True, cmp_pos, cmp_w1, cmp_b1, cmp_w2,
                      k_norm[0].reshape(1, d), cos_c, sin_c)
    v_cmp = _compress(kv_heads(1).reshape(bsz, G, nh, CMP_STRIDE * d), 1, False, cmp_pos, cmp_w1, cmp_b1, cmp_w2,
                      k_norm[0].reshape(1, d), cos_c, sin_c)
    o_cmp, pen = _nsa_cmp(q, k_cmp, v_cmp)

    gw = 16
    gp = jnp.pad(gp.reshape(bsz, t, G, 3 * R), ((0, 0), (0, 0), (0, 0), (0, gw - 3 * R))).transpose(0, 2, 1, 3)
    gb = jnp.pad(gate_bias.reshape(G, 1, 3 * R), ((0, 0), (0, 0), (0, gw - 3 * R)))
    o = _nsa_sel(q, pen, k_sel, v_sel, k_win, v_win, o_cmp, gp, gb)
    return _out_proj(h, [o.reshape(n, -1)], [w_out.astype(BF16)])


def _ffn_weights(wg, wu, wd):
    pad = D_FF_PAD - D_FF
    wg = jnp.pad(wg, ((0, 0), (0, pad))).astype(BF16)
    wu = jnp.pad(wu, ((0, 0), (0, pad))).astype(BF16)
    wd = jnp.pad(wd, ((0, pad), (0, 0))).astype(BF16)
    return wg, wu, wd


def kernel(x, ffn1_norm, ffn1_w_gate, ffn1_w_up, ffn1_w_down, mix_norm, ffn2_norm, ffn2_w_gate, ffn2_w_up,
           ffn2_w_down, ab_w_in, ab_w_out, ml_conv_w, ml_conv_b, ml_wq, ml_wk, ml_i_bias, ml_f_bias, ml_out_norm,
           ml_skip, hg_lb_logits, hg_out_norm, c_w_in, c_w_out, c_q_norm, c_k_norm, c_cmp_pos, c_cmp_w1, c_cmp_b1,
           c_cmp_w2, c_gate_bias):
    bsz, t, d = x.shape
    depth = ffn1_norm.shape[0]
    lb_soft = jax.nn.softmax(hg_lb_logits.astype(F32), axis=0)
    lb_all = jnp.cumsum(lb_soft, axis=0) - lb_soft[0]
    h = x.reshape(bsz * t, d)
    for layer in range(depth):
        h = _ffn(h, ffn1_norm[layer].reshape(1, d),
                 *_ffn_weights(ffn1_w_gate[layer], ffn1_w_up[layer], ffn1_w_down[layer]))
        j = layer // 2
        nw = mix_norm[layer].reshape(1, d)
        if layer % 2 == 0:
            h = _ab_mixer(h, bsz, nw, ab_w_in[j], ab_w_out[j], ml_conv_w[j], ml_conv_b[j], ml_wq[j], ml_wk[j],
                          ml_i_bias[j], ml_f_bias[j], ml_out_norm[j], ml_skip[j], lb_all[j], hg_out_norm[j])
        else:
            h = _nsa_mixer(h, bsz, nw, c_w_in[j], c_w_out[j], c_q_norm[j], c_k_norm[j], c_cmp_pos[j], c_cmp_w1[j],
                           c_cmp_b1[j], c_cmp_w2[j], c_gate_bias[j])
        h = _ffn(h, ffn2_norm[layer].reshape(1, d),
                 *_ffn_weights(ffn2_w_gate[layer], ffn2_w_up[layer], ffn2_w_down[layer]))
    return h.reshape(bsz, t, d)
```

```python
import functools

import numpy as np
import jax
import jax.numpy as jnp
from jax import lax
from jax.experimental import pallas as pl
from jax.experimental.pallas import tpu as pltpu

F32 = jnp.float32
BF16 = jnp.bfloat16

D_MODEL = 2048
D_FF = 5504
NORM_EPS = 1e-6
ROPE_THETA = 10000.0
NEG = -1e30

ML_HEADS = 4
ML_HEAD_DIM = 256
ML_WIDTH = 1024
ML_CONV = 4

HG_HEADS = 8
HG_KEY_DIM = 128
HG_VAL_DIM = 128
HG_KEY_WIDTH = 1024
HG_WIDTH = 1024
HG_MAX_K = 0.999999

NSA_HEADS = 16
NSA_KV_GROUPS = 4
NSA_REP = NSA_HEADS // NSA_KV_GROUPS
NSA_HEAD_DIM = 128
NSA_KV_WIDTH = NSA_KV_GROUPS * NSA_HEAD_DIM
CMP_BLOCK = 32
CMP_STRIDE = 16
CMP_HIDDEN = 256
SEL_BLOCK = 64
SEL_TOPK = 16
WINDOW = 512
FORCE_BONUS = 1e4

LANES = 128
VMEM_LIMIT = 56 * 1024 * 1024

FFN_TM = 1024
FFN_TF = 256
D_FF_PAD = 5632
PROJ_TM = 1024
OUT_TM = 512
PROJ_TN = 512
ML_CHUNK = 128
HG_CHUNK = 128
HG_HEADS_PER_STEP = 2
HG_LEVELS = (8, 16, 32, 64)
NSA_CMP_TQ = 256
NSA_TQ = 256


def _params(sem):
    return pltpu.CompilerParams(dimension_semantics=sem, vmem_limit_bytes=VMEM_LIMIT)


def _rms(x, w):
    ms = jnp.mean(x * x, axis=-1, keepdims=True)
    return x * lax.rsqrt(ms + NORM_EPS) * w


def _log_sigmoid(x):
    return jnp.minimum(x, 0.0) - jnp.log1p(jnp.exp(-jnp.abs(x)))


def _dot(a, b):
    return jnp.dot(a, b, preferred_element_type=F32)


def _dot_nt(a, b):
    return lax.dot_general(a, b, (((1,), (1,)), ((), ())), preferred_element_type=F32)


def _dot_tn(a, b):
    return lax.dot_general(a, b, (((0,), (0,)), ((), ())), preferred_element_type=F32)


def _split3(x):
    hi = x.astype(BF16)
    r1 = x - hi.astype(F32)
    mid = r1.astype(BF16)
    lo = (r1 - mid.astype(F32)).astype(BF16)
    return hi, mid, lo


def _ffn_body(h_ref, nw_ref, wg_ref, wu_ref, wd_ref, o_ref, hn_ref):
    @pl.when(pl.program_id(1) == 0)
    def _():
        x = h_ref[...]
        hn_ref[...] = _rms(x, nw_ref[...]).astype(BF16)
        o_ref[...] = x

    hn = hn_ref[...]
    g = _dot(hn, wg_ref[...])
    u = _dot(hn, wu_ref[...])
    a = (g * jax.nn.sigmoid(g) * u).astype(BF16)
    o_ref[...] += 0.5 * _dot(a, wd_ref[...])


def _ffn(h, nw, wg, wu, wd):
    n, d = h.shape
    fp = wg.shape[1]
    tm = min(FFN_TM, n)
    return pl.pallas_call(
        _ffn_body,
        grid=(n // tm, fp // FFN_TF),
        in_specs=[
            pl.BlockSpec((tm, d), lambda i, j: (i, 0)),
            pl.BlockSpec((1, d), lambda i, j: (0, 0)),
            pl.BlockSpec((d, FFN_TF), lambda i, j: (0, j)),
            pl.BlockSpec((d, FFN_TF), lambda i, j: (0, j)),
            pl.BlockSpec((FFN_TF, d), lambda i, j: (j, 0)),
        ],
        out_specs=pl.BlockSpec((tm, d), lambda i, j: (i, 0)),
        out_shape=jax.ShapeDtypeStruct((n, d), F32),
        scratch_shapes=[pltpu.VMEM((tm, d), BF16)],
        compiler_params=_params(("arbitrary", "arbitrary")),
    )(h, nw, wg, wu, wd)


def _norm_proj_body(h_ref, nw_ref, w_ref, o_ref, hn_ref):
    @pl.when(pl.program_id(1) == 0)
    def _():
        hn_ref[...] = _rms(h_ref[...], nw_ref[...]).astype(BF16)

    o_ref[...] = _dot(hn_ref[...], w_ref[...])


def _norm_proj(h, nw, w):
    n, d = h.shape
    c = w.shape[1]
    tm = min(PROJ_TM, n)
    tn = min(PROJ_TN, c)
    return pl.pallas_call(
        _norm_proj_body,
        grid=(n // tm, c // tn),
        in_specs=[
            pl.BlockSpec((tm, d), lambda i, j: (i, 0)),
            pl.BlockSpec((1, d), lambda i, j: (0, 0)),
            pl.BlockSpec((d, tn), lambda i, j: (0, j)),
        ],
        out_specs=pl.BlockSpec((tm, tn), lambda i, j: (i, j)),
        out_shape=jax.ShapeDtypeStruct((n, c), F32),
        scratch_shapes=[pltpu.VMEM((tm, d), BF16)],
        compiler_params=_params(("arbitrary", "arbitrary")),
    )(h, nw, w)


def _out_proj_body(n_in, h_ref, *refs):
    y_refs, w_refs, o_ref = refs[:n_in], refs[n_in:2 * n_in], refs[2 * n_in]
    acc = h_ref[...]
    for y_ref, w_ref in zip(y_refs, w_refs):
        acc = acc + _dot(y_ref[...], w_ref[...])
    o_ref[...] = acc


def _out_proj(h, ys, ws):
    n, d = h.shape
    tm = min(OUT_TM, n)
    in_specs = [pl.BlockSpec((tm, d), lambda i: (i, 0))]
    in_specs += [pl.BlockSpec((tm, y.shape[1]), lambda i: (i, 0)) for y in ys]
    in_specs += [pl.BlockSpec(w.shape, lambda i: (0, 0)) for w in ws]
    return pl.pallas_call(
        functools.partial(_out_proj_body, len(ys)),
        grid=(n // tm,),
        in_specs=in_specs,
        out_specs=pl.BlockSpec((tm, d), lambda i: (i, 0)),
        out_shape=jax.ShapeDtypeStruct((n, d), F32),
        compiler_params=_params(("arbitrary",)),
    )(h, *ys, *ws)


def _mlstm_body(bias_ref, u_ref, v_ref, og_ref, ig_ref, fg_ref, cw_ref, cb_ref, wq_ref, wk_ref, nw_ref, sk_ref,
                y_ref, ct_ref, n_ref, m_ref, up_ref):
    hh = pl.program_id(1)
    L, dh = u_ref.shape[1], u_ref.shape[2]

    @pl.when(pl.program_id(2) == 0)
    def _():
        ct_ref[...] = jnp.zeros_like(ct_ref)
        n_ref[...] = jnp.zeros_like(n_ref)
        m_ref[...] = jnp.zeros_like(m_ref)
        up_ref[...] = jnp.zeros_like(up_ref)

    u = u_ref[0]
    up = up_ref[...]
    row = lax.broadcasted_iota(jnp.int32, (L, dh), 0)
    cw = cw_ref[...]
    acc = u * cw[ML_CONV - 1:ML_CONV, :] + cb_ref[...]
    for j in range(1, ML_CONV):
        shifted = jnp.where(row < j, pltpu.roll(up, j, 0), pltpu.roll(u, j, 0))
        acc = acc + shifted * cw[ML_CONV - 1 - j:ML_CONV - j, :]
    up_ref[...] = u
    c = acc * jax.nn.sigmoid(acc)
    c16 = c.astype(BF16)
    q = _dot(c16, wq_ref[0])
    k = _dot(c16, wk_ref[0]) * (dh ** -0.5)
    q16, k16, v16 = q.astype(BF16), k.astype(BF16), v_ref[0].astype(BF16)

    f_row = _log_sigmoid(fg_ref[0, 0] + bias_ref[ML_HEADS + hh])
    i_row = ig_ref[0, 0] + bias_ref[hh]
    li = lax.broadcasted_iota(jnp.int32, (L, L), 0)
    si = lax.broadcasted_iota(jnp.int32, (L, L), 1)
    tri = si <= li
    eye = si == li
    f_b = jnp.broadcast_to(f_row, (L, L))
    i_b = jnp.broadcast_to(i_row, (L, L))
    b_col = jnp.sum(jnp.where(tri, f_b, 0.0), axis=1, keepdims=True)
    f_col = jnp.sum(jnp.where(eye, f_b, 0.0), axis=1, keepdims=True)
    i_col = jnp.sum(jnp.where(eye, i_b, 0.0), axis=1, keepdims=True)
    b_row = jnp.sum(jnp.where(li <= si, jnp.broadcast_to(f_col, (L, L)), 0.0), axis=0, keepdims=True)
    a_row = i_row - b_row
    a_col = i_col - b_col
    m_old = m_ref[...]
    mrow = jnp.maximum(m_old, jnp.max(jnp.where(tri, jnp.broadcast_to(a_row, (L, L)), NEG), axis=1, keepdims=True))
    w_intra = jnp.exp(jnp.where(tri, a_row - mrow, NEG))
    w_inter = jnp.exp(m_old - mrow)
    qk = _dot_nt(q16, k16) * w_intra
    num = _dot(qk.astype(BF16), v16) + w_inter * _dot(q16, ct_ref[...].astype(BF16))
    den = jnp.sum(qk, axis=1, keepdims=True) + w_inter * jnp.sum(q * n_ref[...], axis=1, keepdims=True)
    hout = num / jnp.maximum(jnp.abs(den), jnp.exp(-(b_col + mrow)))

    amax = jnp.maximum(m_old, jnp.max(a_row, axis=1, keepdims=True))
    g = jnp.sum(f_row, axis=1, keepdims=True)
    a_prev = jnp.exp(m_old - amax)
    ks = k * jnp.exp(a_col - amax)
    ct_ref[...] = a_prev * ct_ref[...] + _dot_tn(ks.astype(BF16), v16)
    n_ref[...] = a_prev * n_ref[...] + jnp.sum(ks, axis=0, keepdims=True)
    m_ref[...] = g + amax

    hm = _rms(hout, nw_ref[...])
    y_ref[0] = (jax.nn.sigmoid(og_ref[0]) * (hm + sk_ref[...] * c)).astype(BF16)


def _mlstm(proj, gates, gate_bias, conv_w, conv_b, wq, wk, norm_w, skip):
    b, t, _ = proj.shape
    L, dh, nh = ML_CHUNK, ML_HEAD_DIM, ML_HEADS
    vec = lambda: pl.BlockSpec((1, dh), lambda bi, hi, ci: (0, hi))
    return pl.pallas_call(
        _mlstm_body,
        grid=(b, nh, t // L),
        in_specs=[
            pl.BlockSpec(memory_space=pltpu.SMEM),
            pl.BlockSpec((1, L, dh), lambda bi, hi, ci: (bi, ci, hi)),
            pl.BlockSpec((1, L, dh), lambda bi, hi, ci: (bi, ci, nh + hi)),
            pl.BlockSpec((1, L, dh), lambda bi, hi, ci: (bi, ci, 2 * nh + hi)),
            pl.BlockSpec((1, 1, 1, L), lambda bi, hi, ci: (bi, hi, 0, ci)),
            pl.BlockSpec((1, 1, 1, L), lambda bi, hi, ci: (bi, nh + hi, 0, ci)),
            pl.BlockSpec((ML_CONV, dh), lambda bi, hi, ci: (0, hi)),
            vec(),
            pl.BlockSpec((1, dh, dh), lambda bi, hi, ci: (hi, 0, 0)),
            pl.BlockSpec((1, dh, dh), lambda bi, hi, ci: (hi, 0, 0)),
            vec(),
            vec(),
        ],
        out_specs=pl.BlockSpec((1, L, dh), lambda bi, hi, ci: (bi, ci, hi)),
        out_shape=jax.ShapeDtypeStruct((b, t, ML_WIDTH), BF16),
        scratch_shapes=[
            pltpu.VMEM((dh, dh), F32),
            pltpu.VMEM((1, dh), F32),
            pltpu.VMEM((1, 1), F32),
            pltpu.VMEM((L, dh), F32),
        ],
        compiler_params=_params(("arbitrary", "arbitrary", "arbitrary")),
    )(gate_bias, proj, proj, proj, gates, gates, conv_w, conv_b, wq, wk, norm_w, skip)


def _hg_level_tables(L):
    l = np.arange(L)[:, None]
    s = np.arange(L)[None, :]
    mats = [(s <= l)]
    mats.append(s <= (l // 8) * 8 + 3)
    for h in HG_LEVELS:
        mats.append(s <= (l // (2 * h)) * (2 * h) + h - 1)
    stack = np.concatenate(mats, axis=0).astype(np.float32)
    lvl = np.full((L, L), -1.0, np.float32)
    lvl[(l // 8 == s // 8) & (s <= l)] = 0.0
    for idx, h in enumerate(HG_LEVELS):
        sel = (l // (2 * h) == s // (2 * h)) & (l % (2 * h) >= h) & (s % (2 * h) < h)
        lvl[sel] = idx + 1.0
    return stack, lvl


def _hgrn2_head(hq, z, vi, hg, lb, nw, stack, lvl, st):
    L = hq.shape[0]
    k = (1.0 - lb) * jax.nn.sigmoid(-z)
    log_f = jnp.maximum(jnp.log1p(-jnp.minimum(k, HG_MAX_K)), _log_sigmoid(z))
    q = hq * jax.nn.sigmoid(hq)
    v16 = vi.astype(BF16)

    hi, mid, lo = _split3(log_f)
    br = _dot(stack, hi) + _dot(stack, mid) + _dot(stack, lo)
    b = br[0:L]
    r0 = br[L:2 * L]
    att = jnp.where(lvl == 0.0, _dot_nt((q * jnp.exp(b - r0)).astype(BF16), (k * jnp.exp(r0 - b)).astype(BF16)), 0.0)
    for idx in range(len(HG_LEVELS)):
        r = br[(idx + 2) * L:(idx + 3) * L]
        qd = (q * jnp.exp(jnp.minimum(b - r, 0.0))).astype(BF16)
        kd = (k * jnp.exp(jnp.minimum(r - b, 0.0))).astype(BF16)
        att = att + jnp.where(lvl == idx + 1.0, _dot_nt(qd, kd), 0.0)

    o = _dot(att.astype(BF16), v16) + _dot_nt((q * jnp.exp(b)).astype(BF16), st.astype(BF16))
    g = b[L - 1:L, :]
    st_new = st * jnp.exp(g) + _dot_tn(v16, (k * jnp.exp(g - b)).astype(BF16))
    y = (_rms(o, nw) * (hg * jax.nn.sigmoid(hg))).astype(BF16)
    return y, st_new


def _hgrn2_body(q_ref, f_ref, i_ref, g_ref, lb_ref, nw_ref, stack_ref, lvl_ref, y_ref, st_ref):
    @pl.when(pl.program_id(2) == 0)
    def _():
        st_ref[...] = jnp.zeros_like(st_ref)

    stack = stack_ref[...]
    lvl = lvl_ref[...]
    dk = HG_KEY_DIM
    for hh in range(st_ref.shape[0]):
        cs = slice(hh * dk, (hh + 1) * dk)
        y, st = _hgrn2_head(q_ref[0, :, cs], f_ref[0, :, cs], i_ref[0, :, cs], g_ref[0, :, cs], lb_ref[:, cs],
                            nw_ref[:, cs], stack, lvl, st_ref[hh])
        st_ref[hh] = st
        y_ref[0, :, cs] = y


def _hgrn2(proj, lb, norm_w):
    b, t, _ = proj.shape
    L, dk, nh = HG_CHUNK, HG_KEY_DIM, HG_HEADS
    hps = HG_HEADS_PER_STEP
    wd = hps * dk
    stack, lvl = _hg_level_tables(L)
    base = 3 * ML_WIDTH // wd
    ng = nh // hps
    col = lambda off: pl.BlockSpec((1, L, wd), lambda bi, hi, ci: (bi, ci, base + off * ng + hi))
    vec = lambda: pl.BlockSpec((1, wd), lambda bi, hi, ci: (0, hi))
    return pl.pallas_call(
        _hgrn2_body,
        grid=(b, ng, t // L),
        in_specs=[
            col(0), col(1), col(2), col(3), vec(), vec(),
            pl.BlockSpec(stack.shape, lambda bi, hi, ci: (0, 0)),
            pl.BlockSpec(lvl.shape, lambda bi, hi, ci: (0, 0)),
        ],
        out_specs=pl.BlockSpec((1, L, wd), lambda bi, hi, ci: (bi, ci, hi)),
        out_shape=jax.ShapeDtypeStruct((b, t, HG_WIDTH), BF16),
        scratch_shapes=[pltpu.VMEM((hps, HG_VAL_DIM, dk), F32)],
        compiler_params=_params(("arbitrary", "arbitrary", "arbitrary")),
    )(proj, proj, proj, proj, lb, norm_w, jnp.asarray(stack, BF16), jnp.asarray(lvl))


def _rope_tables(pos):
    half = NSA_HEAD_DIM // 2
    inv_freq = jnp.power(ROPE_THETA, -jnp.arange(half, dtype=F32) / half)
    ang = pos.astype(F32)[:, None] * inv_freq[None, :]
    cos, sin = jnp.cos(ang), jnp.sin(ang)
    return jnp.concatenate([cos, cos], axis=1), jnp.concatenate([-sin, sin], axis=1)


def _rope(x, cos, sin_signed):
    return x * cos + pltpu.roll(x, NSA_HEAD_DIM // 2, 1) * sin_signed


def _head_prep_body(scale, x_ref, nw_ref, cos_ref, sin_ref, o_ref):
    y = _rope(_rms(x_ref[0], nw_ref[...]), cos_ref[...], sin_ref[...])
    o_ref[0, 0] = (y * scale).astype(BF16)


def _head_prep(proj, col0, nheads, nw, cos, sin, scale):
    b, t, _ = proj.shape
    d = NSA_HEAD_DIM
    tt = min(512, t)
    return pl.pallas_call(
        functools.partial(_head_prep_body, scale),
        grid=(b, nheads, t // tt),
        in_specs=[
            pl.BlockSpec((1, tt, d), lambda bi, hi, ti: (bi, ti, col0 + hi)),
            pl.BlockSpec((1, d), lambda bi, hi, ti: (0, 0)),
            pl.BlockSpec((tt, d), lambda bi, hi, ti: (ti, 0)),
            pl.BlockSpec((tt, d), lambda bi, hi, ti: (ti, 0)),
        ],
        out_specs=pl.BlockSpec((1, 1, tt, d), lambda bi, hi, ti: (bi, hi, ti, 0)),
        out_shape=jax.ShapeDtypeStruct((b, nheads, t, d), BF16),
        compiler_params=_params(("arbitrary", "arbitrary", "arbitrary")),
    )(proj, nw, cos, sin)


def _compress_body(do_norm, x_ref, pt_ref, pb_ref, w1t_ref, w1b_ref, b1_ref, w2_ref, nw_ref, cos_ref, sin_ref, o_ref):
    x = x_ref[0, 0]
    nh = x.shape[0]
    top = _dot((x + pt_ref[0]).astype(BF16), w1t_ref[0])
    bot = _dot((x + pb_ref[0]).astype(BF16), w1b_ref[0])
    hid = top + pltpu.roll(bot, nh - 1, 0) + b1_ref[0]
    hid = hid * jax.nn.sigmoid(hid)
    y = _dot(hid.astype(BF16), w2_ref[0])
    if do_norm:
        y = _rope(_rms(y, nw_ref[...]), cos_ref[...], sin_ref[...])
    o_ref[0, 0] = y.astype(BF16)


def _compress(x, which, do_norm, pos, w1, b1, w2, nw, cos, sin):
    b, g, nh, wd = x.shape
    d = NSA_HEAD_DIM
    half = CMP_STRIDE * d
    pos_t = pos[:, :CMP_STRIDE].reshape(2, 1, half)
    pos_b = pos[:, CMP_STRIDE:].reshape(2, 1, half)
    w1t = w1[:, :half].astype(BF16)
    w1b = w1[:, half:].astype(BF16)
    sel = lambda bi, gi: (which, 0, 0)
    return pl.pallas_call(
        functools.partial(_compress_body, do_norm),
        grid=(b, g),
        in_specs=[
            pl.BlockSpec((1, 1, nh, wd), lambda bi, gi: (bi, gi, 0, 0)),
            pl.BlockSpec((1, 1, half), sel),
            pl.BlockSpec((1, 1, half), sel),
            pl.BlockSpec((1, half, CMP_HIDDEN), sel),
            pl.BlockSpec((1, half, CMP_HIDDEN), sel),
            pl.BlockSpec((1, 1, CMP_HIDDEN), sel),
            pl.BlockSpec((1, CMP_HIDDEN, d), sel),
            pl.BlockSpec((1, d), lambda bi, gi: (0, 0)),
            pl.BlockSpec((nh, d), lambda bi, gi: (0, 0)),
            pl.BlockSpec((nh, d), lambda bi, gi: (0, 0)),
        ],
        out_specs=pl.BlockSpec((1, 1, nh, d), lambda bi, gi: (bi, gi, 0, 0)),
        out_shape=jax.ShapeDtypeStruct((b, g, nh, d), BF16),
        compiler_params=_params(("arbitrary", "arbitrary")),
    )(x, pos_t, pos_b, w1t, w1b, b1.reshape(2, 1, CMP_HIDDEN), w2.astype(BF16), nw, cos, sin)


def _nsa_cmp_body(q_ref, kc_ref, vc_ref, cov_ref, oc_ref, pen_ref, imp_ref):
    R, TQ, d = q_ref.shape[2], q_ref.shape[3], q_ref.shape[4]
    ncp = kc_ref.shape[2]
    ns = cov_ref.shape[0]
    t0 = pl.program_id(2) * TQ
    q4 = q_ref[0, 0].reshape(R * TQ, d)
    s = _dot_nt(kc_ref[0, 0], q4)
    n_i = lax.broadcasted_iota(jnp.int32, (ncp, R * TQ), 0)
    t_i = t0 + (lax.broadcasted_iota(jnp.int32, (ncp, R * TQ), 1) & (TQ - 1))
    valid = n_i * CMP_STRIDE + (CMP_BLOCK - 1) <= t_i
    s = jnp.where(valid, s, NEG)
    e = jnp.where(valid, jnp.exp(s - jnp.max(s, axis=0, keepdims=True)), 0.0)
    zsum = jnp.sum(e, axis=0, keepdims=True)
    p = e / jnp.where(zsum > 0.0, zsum, 1.0)
    oc_ref[0, 0] = _dot_tn(p.astype(BF16), vc_ref[0, 0]).reshape(R, TQ, d)

    psum = p[:, 0:TQ]
    for r in range(1, R):
        psum = psum + p[:, r * TQ:(r + 1) * TQ]
    p_hi = psum.astype(BF16)
    p_lo = (psum - p_hi.astype(F32)).astype(BF16)
    cov = cov_ref[...]
    imp = _dot(cov, p_hi) + _dot(cov, p_lo)
    blk = lax.broadcasted_iota(jnp.int32, (ns, TQ), 0)
    tt = t0 + lax.broadcasted_iota(jnp.int32, (ns, TQ), 1)
    cur = lax.shift_right_logical(tt, int(np.log2(SEL_BLOCK)))
    forced = (blk == 0) | (blk == cur) | (blk == cur - 1)
    imp = jnp.where(blk * SEL_BLOCK <= tt, imp + jnp.where(forced, FORCE_BONUS, 0.0), NEG)
    imp_ref[...] = imp

    def count(j, rank):
        x = imp_ref[pl.ds(j, 1), :]
        ge = jnp.where(x >= imp, 1.0, 0.0)
        gt = jnp.where(x > imp, 1.0, 0.0)
        return rank + jnp.where(blk > j, ge, gt)

    n_causal = jnp.minimum((t0 + TQ + SEL_BLOCK - 1) // SEL_BLOCK, ns)
    rank = lax.fori_loop(0, n_causal, count, jnp.zeros((ns, TQ), F32))
    pen_t = jnp.where(rank < float(min(SEL_TOPK, ns)), 0.0, NEG)
    pen_ref[0, 0] = pen_t.T.astype(BF16)


def _nsa_cmp(q, k_cmp, v_cmp):
    b, g, r, t, d = q.shape
    ncp = k_cmp.shape[2]
    ns = t // SEL_BLOCK
    tq = min(NSA_CMP_TQ, t)
    ci = np.arange(ncp)[None, :] * CMP_STRIDE
    sj = np.arange(ns)[:, None] * SEL_BLOCK
    cover_t = ((ci < sj + SEL_BLOCK) & (ci + CMP_BLOCK > sj) & (np.arange(ncp)[None, :] < ncp - 1))
    return pl.pallas_call(
        _nsa_cmp_body,
        grid=(b, g, t // tq),
        in_specs=[
            pl.BlockSpec((1, 1, r, tq, d), lambda bi, gi, qi: (bi, gi, 0, qi, 0)),
            pl.BlockSpec((1, 1, ncp, d), lambda bi, gi, qi: (bi, gi, 0, 0)),
            pl.BlockSpec((1, 1, ncp, d), lambda bi, gi, qi: (bi, gi, 0, 0)),
            pl.BlockSpec((ns, ncp), lambda bi, gi, qi: (0, 0)),
        ],
        out_specs=[
            pl.BlockSpec((1, 1, r, tq, d), lambda bi, gi, qi: (bi, gi, 0, qi, 0)),
            pl.BlockSpec((1, 1, tq, ns), lambda bi, gi, qi: (bi, gi, qi, 0)),
        ],
        out_shape=[
            jax.ShapeDtypeStruct((b, g, r, t, d), F32),
            jax.ShapeDtypeStruct((b, g, t, ns), BF16),
        ],
        scratch_shapes=[pltpu.VMEM((ns, tq), F32)],
        compiler_params=_params(("arbitrary", "arbitrary", "arbitrary")),
    )(q, k_cmp, v_cmp, jnp.asarray(cover_t.astype(np.float32), BF16))


def _flash_step(s, v16, m_ref, l_ref, acc_ref):
    m_prev = m_ref[...]
    m_next = jnp.maximum(m_prev, jnp.max(functools.reduce(jnp.maximum, s), axis=1, keepdims=True))
    alpha = jnp.exp(m_prev - m_next)
    p = [jnp.exp(sc - m_next) for sc in s]
    l_ref[...] = alpha * l_ref[...] + jnp.sum(functools.reduce(jnp.add, p), axis=1, keepdims=True)
    p16 = jnp.concatenate([pc.astype(BF16) for pc in p], axis=1)
    acc_ref[...] = alpha * acc_ref[...] + _dot(p16, v16)
    m_ref[...] = m_next


def _col_groups(s):
    return [s[:, c * LANES:(c + 1) * LANES] for c in range(s.shape[1] // LANES)]


def _nsa_sel_body(q_ref, pen_ref, ks_ref, e_ref, vs_ref, kw_ref, vw_ref, oc_ref, gp_ref, gb_ref, o_ref,
                  m1_ref, l1_ref, a1_ref, m2_ref, l2_ref, a2_ref):
    R, TQ, d = q_ref.shape[2], q_ref.shape[3], q_ref.shape[4]
    rows = R * TQ
    qi = pl.program_id(2)
    q4 = q_ref[0, 0].reshape(rows, d)
    pen = pen_ref[0, 0]
    qa = jnp.concatenate([q4, jnp.concatenate([pen] * R, axis=0)], axis=1)
    t_rel = lax.broadcasted_iota(jnp.int32, (rows, LANES), 0) & (TQ - 1)
    lane = lax.broadcasted_iota(jnp.int32, (rows, LANES), 1)

    for m_ref, l_ref, a_ref in ((m1_ref, l1_ref, a1_ref), (m2_ref, l2_ref, a2_ref)):
        m_ref[...] = jnp.full_like(m_ref, NEG)
        l_ref[...] = jnp.zeros_like(l_ref)
        a_ref[...] = jnp.zeros_like(a_ref)

    def sel_scores(kt):
        k0 = pl.multiple_of(kt * TQ, TQ)
        ka = jnp.concatenate([ks_ref[0, 0, pl.ds(k0, TQ), :], e_ref[pl.ds(k0, TQ), :]], axis=1)
        return _col_groups(_dot_nt(qa, ka)), vs_ref[0, 0, pl.ds(k0, TQ), :]

    def win_scores(kt):
        k0 = pl.multiple_of(kt * TQ, TQ)
        return _col_groups(_dot_nt(q4, kw_ref[0, 0, pl.ds(k0, TQ), :])), vw_ref[0, 0, pl.ds(k0, TQ), :]

    def causal(s):
        return [jnp.where(c * LANES + lane <= t_rel, sc, NEG) for c, sc in enumerate(s)]

    s, v = sel_scores(qi)
    _flash_step(causal(s), v, m1_ref, l1_ref, a1_ref)

    def sel_pair(i, carry):
        sa, va = sel_scores(2 * i)
        sb, vb = sel_scores(2 * i + 1)
        _flash_step(sa, va, m1_ref, l1_ref, a1_ref)
        _flash_step(sb, vb, m1_ref, l1_ref, a1_ref)
        return carry

    lax.fori_loop(0, lax.shift_right_logical(qi, 1), sel_pair, 0)

    @pl.when((qi & 1) == 1)
    def _():
        s, v = sel_scores(qi - 1)
        _flash_step(s, v, m1_ref, l1_ref, a1_ref)

    s, v = win_scores(qi)
    _flash_step(causal(s), v, m2_ref, l2_ref, a2_ref)
    n_back = WINDOW // TQ

    def win_step(kt, carry):
        s, v = win_scores(kt)
        _flash_step(s, v, m2_ref, l2_ref, a2_ref)
        return carry

    lax.fori_loop(jnp.maximum(qi - n_back + 1, 0), qi, win_step, 0)

    @pl.when(qi >= n_back)
    def _():
        s, v = win_scores(qi - n_back)
        s = [jnp.where(c * LANES + lane > t_rel, sc, NEG) for c, sc in enumerate(s)]
        _flash_step(s, v, m2_ref, l2_ref, a2_ref)

    gate = jax.nn.sigmoid(gp_ref[0, 0] + gb_ref[0])
    o_sel = a1_ref[...] / l1_ref[...]
    o_win = a2_ref[...] / l2_ref[...]
    for r in range(R):
        rs = slice(r * TQ, (r + 1) * TQ)
        o = (gate[:, 3 * r:3 * r + 1] * oc_ref[0, 0, r]
             + gate[:, 3 * r + 1:3 * r + 2] * o_sel[rs]
             + gate[:, 3 * r + 2:3 * r + 3] * o_win[rs])
        o_ref[0, :, r * d:(r + 1) * d] = o.astype(BF16)


def _nsa_sel(q, pen, k_sel, v_sel, k_win, v_win, o_cmp, gate_pre, gate_bias):
    b, g, r, t, d = q.shape
    ns = pen.shape[3]
    tq = min(NSA_TQ, t)
    assert WINDOW % tq == 0 and t % tq == 0 and d == LANES
    rows = r * tq
    onehot = (np.arange(t)[:, None] // SEL_BLOCK == np.arange(ns)[None, :]).astype(np.float32)
    kv = lambda: pl.BlockSpec((1, 1, t, d), lambda bi, gi, qi: (bi, gi, 0, 0))
    qspec = lambda: pl.BlockSpec((1, 1, r, tq, d), lambda bi, gi, qi: (bi, gi, 0, qi, 0))
    gw = gate_pre.shape[3]
    stat = lambda: pltpu.VMEM((rows, LANES), F32)
    return pl.pallas_call(
        _nsa_sel_body,
        grid=(b, g, t // tq),
        in_specs=[
            qspec(),
            pl.BlockSpec((1, 1, tq, ns), lambda bi, gi, qi: (bi, gi, qi, 0)),
            kv(),
            pl.BlockSpec((t, ns), lambda bi, gi, qi: (0, 0)),
            kv(), kv(), kv(),
            qspec(),
            pl.BlockSpec((1, 1, tq, gw), lambda bi, gi, qi: (bi, gi, qi, 0)),
            pl.BlockSpec((1, 1, gw), lambda bi, gi, qi: (gi, 0, 0)),
        ],
        out_specs=pl.BlockSpec((1, tq, r * d), lambda bi, gi, qi: (bi, qi, gi)),
        out_shape=jax.ShapeDtypeStruct((b, t, g * r * d), BF16),
        scratch_shapes=[stat(), stat(), stat(), stat(), stat(), stat()],
        compiler_params=_params(("arbitrary", "arbitrary", "arbitrary")),
    )(q, pen, k_sel, jnp.asarray(onehot, BF16), v_sel, k_win, v_win, o_cmp, gate_pre, gate_bias)


def _pad_cols(w, mult):
    pad = (-w.shape[1]) % mult
    return jnp.pad(w, ((0, 0), (0, pad))) if pad else w


def _ab_mixer(h, bsz, nw, w_in, w_out, conv_w, conv_b, wq, wk, i_bias, f_bias, ml_norm, ml_skip, lb, hg_norm):
    n, _ = h.shape
    t = n // bsz
    g0 = 3 * ML_WIDTH
    g1 = g0 + 2 * ML_HEADS
    w_main = jnp.concatenate([w_in[:, :g0], w_in[:, g1:]], axis=1).astype(BF16)
    w_gate = _pad_cols(w_in[:, g0:g1], LANES).astype(BF16)
    proj = _norm_proj(h, nw, w_main).reshape(bsz, t, -1)
    gates = _norm_proj(h, nw, w_gate)[:, :2 * ML_HEADS]
    gates = gates.reshape(bsz, t, 2 * ML_HEADS).transpose(0, 2, 1).reshape(bsz, 2 * ML_HEADS, 1, t)
    y_ml = _mlstm(proj, gates, jnp.concatenate([i_bias, f_bias]), conv_w, conv_b.reshape(1, -1),
                  wq.astype(BF16), wk.astype(BF16), ml_norm.reshape(1, -1), ml_skip.reshape(1, -1))
    y_hg = _hgrn2(proj, lb.reshape(1, -1), hg_norm.reshape(1, -1))
    w_out16 = w_out.astype(BF16)
    return _out_proj(h, [y_ml.reshape(n, -1), y_hg.reshape(n, -1)], [w_out16[:ML_WIDTH], w_out16[ML_WIDTH:]])


def _nsa_mixer(h, bsz, nw, w_in, w_out, q_norm, k_norm, cmp_pos, cmp_w1, cmp_b1, cmp_w2, gate_bias):
    n, _ = h.shape
    t = n // bsz
    d, G, R = NSA_HEAD_DIM, NSA_KV_GROUPS, NSA_REP
    c_main = NSA_HEADS * d + 6 * NSA_KV_WIDTH
    proj = _norm_proj(h, nw, w_in[:, :c_main].astype(BF16)).reshape(bsz, t, c_main)
    gp = _norm_proj(h, nw, _pad_cols(w_in[:, c_main:], LANES).astype(BF16))[:, :3 * NSA_HEADS]

    cos, sin = _rope_tables(jnp.arange(t))
    nh = t // CMP_STRIDE
    cos_c, sin_c = _rope_tables(jnp.arange(nh) * CMP_STRIDE + CMP_BLOCK - 1)
    q = _head_prep(proj, 0, NSA_HEADS, q_norm.reshape(1, d), cos, sin, d ** -0.5).reshape(bsz, G, R, t, d)
    kvb = NSA_HEADS
    k_sel = _head_prep(proj, kvb + 2 * G, G, k_norm[1].reshape(1, d), cos, sin, 1.0)
    k_win = _head_prep(proj, kvb + 4 * G, G, k_norm[2].reshape(1, d), cos, sin, 1.0)

    def kv_heads(idx):
        c0 = NSA_HEADS * d + idx * NSA_KV_WIDTH
        return proj[:, :, c0:c0 + NSA_KV_WIDTH].reshape(bsz, t, G, d).transpose(0, 2, 1, 3)

    v_sel = kv_heads(3).astype(BF16)
    v_win = kv_heads(5).astype(BF16)
    k_cmp = _compress(kv_heads(0).reshape(bsz, G, nh, CMP_STRIDE * d), 0, True, cmp_pos, cmp_w1, cmp_b1, cmp_w2,
                      k_norm[0].reshape(1, d), cos_c, sin_c)
    v_cmp = _compress(kv_heads(1).reshape(bsz, G, nh, CMP_STRIDE * d), 1, False, cmp_pos, cmp_w1, cmp_b1, cmp_w2,
                      k_norm[0].reshape(1, d), cos_c, sin_c)
    o_cmp, pen = _nsa_cmp(q, k_cmp, v_cmp)

    gw = 16
    gp = jnp.pad(gp.reshape(bsz, t, G, 3 * R), ((0, 0), (0, 0), (0, 0), (0, gw - 3 * R))).transpose(0, 2, 1, 3)
    gb = jnp.pad(gate_bias.reshape(G, 1, 3 * R), ((0, 0), (0, 0), (0, gw - 3 * R)))
    o = _nsa_sel(q, pen, k_sel, v_sel, k_win, v_win, o_cmp, gp, gb)
    return _out_proj(h, [o.reshape(n, -1)], [w_out.astype(BF16)])


def _ffn_weights(wg, wu, wd):
    pad = D_FF_PAD - D_FF
    wg = jnp.pad(wg, ((0, 0), (0, pad))).astype(BF16)
    wu = jnp.pad(wu, ((0, 0), (0, pad))).astype(BF16)
    wd = jnp.pad(wd, ((0, pad), (0, 0))).astype(BF16)
    return wg, wu, wd


def kernel(x, ffn1_norm, ffn1_w_gate, ffn1_w_up, ffn1_w_down, mix_norm, ffn2_norm, ffn2_w_gate, ffn2_w_up,
           ffn2_w_down, ab_w_in, ab_w_out, ml_conv_w, ml_conv_b, ml_wq, ml_wk, ml_i_bias, ml_f_bias, ml_out_norm,
           ml_skip, hg_lb_logits, hg_out_norm, c_w_in, c_w_out, c_q_norm, c_k_norm, c_cmp_pos, c_cmp_w1, c_cmp_b1,
           c_cmp_w2, c_gate_bias):
    bsz, t, d = x.shape
    depth = ffn1_norm.shape[0]
    lb_soft = jax.nn.softmax(hg_lb_logits.astype(F32), axis=0)
    lb_all = jnp.cumsum(lb_soft, axis=0) - lb_soft[0]
    h = x.reshape(bsz * t, d)
    for layer in range(depth):
        h = _ffn(h, ffn1_norm[layer].reshape(1, d),
                 *_ffn_weights(ffn1_w_gate[layer], ffn1_w_up[layer], ffn1_w_down[layer]))
        j = layer // 2
        nw = mix_norm[layer].reshape(1, d)
        if layer % 2 == 0:
            h = _ab_mixer(h, bsz, nw, ab_w_in[j], ab_w_out[j], ml_conv_w[j], ml_conv_b[j], ml_wq[j], ml_wk[j],
                          ml_i_bias[j], ml_f_bias[j], ml_out_norm[j], ml_skip[j], lb_all[j], hg_out_norm[j])
        else:
            h = _nsa_mixer(h, bsz, nw, c_w_in[j], c_w_out[j], c_q_norm[j], c_k_norm[j], c_cmp_pos[j], c_cmp_w1[j],
                           c_cmp_b1[j], c_cmp_w2[j], c_gate_bias[j])
        h = _ffn(h, ffn2_norm[layer].reshape(1, d),
                 *_ffn_weights(ffn2_w_gate[layer], ffn2_w_up[layer], ffn2_w_down[layer]))
    return h.reshape(bsz, t, d)
```

```python
import functools

import numpy as np
import jax
import jax.numpy as jnp
from jax import lax
from jax.experimental import pallas as pl
from jax.experimental.pallas import tpu as pltpu

F32 = jnp.float32
BF16 = jnp.bfloat16

D_MODEL = 2048
D_FF = 5504
NORM_EPS = 1e-6
ROPE_THETA = 10000.0
NEG = -1e30

ML_HEADS = 4
ML_HEAD_DIM = 256
ML_WIDTH = 1024
ML_CONV = 4

HG_HEADS = 8
HG_KEY_DIM = 128
HG_VAL_DIM = 128
HG_KEY_WIDTH = 1024
HG_WIDTH = 1024
HG_MAX_K = 0.999999

NSA_HEADS = 16
NSA_KV_GROUPS = 4
NSA_REP = NSA_HEADS // NSA_KV_GROUPS
NSA_HEAD_DIM = 128
NSA_KV_WIDTH = NSA_KV_GROUPS * NSA_HEAD_DIM
CMP_BLOCK = 32
CMP_STRIDE = 16
CMP_HIDDEN = 256
SEL_BLOCK = 64
SEL_TOPK = 16
WINDOW = 512
FORCE_BONUS = 1e4

LANES = 128
VMEM_LIMIT = 56 * 1024 * 1024

FFN_TM = 1024
FFN_TF = 256
D_FF_PAD = 5632
PROJ_TM = 1024
OUT_TM = 512
PROJ_TN = 512
ML_CHUNK = 128
HG_CHUNK = 128
HG_HEADS_PER_STEP = 2
HG_LEVELS = (8, 16, 32, 64)
PREP_TT = 2048
NSA_CMP_TQ = 256
NSA_TQ = 256


def _params(sem):
    return pltpu.CompilerParams(dimension_semantics=sem, vmem_limit_bytes=VMEM_LIMIT)


def _rms(x, w):
    ms = jnp.mean(x * x, axis=-1, keepdims=True)
    return x * lax.rsqrt(ms + NORM_EPS) * w


def _log_sigmoid(x):
    return jnp.minimum(x, 0.0) - jnp.log1p(jnp.exp(-jnp.abs(x)))


def _dot(a, b):
    return jnp.dot(a, b, preferred_element_type=F32)


def _dot_nt(a, b):
    return lax.dot_general(a, b, (((1,), (1,)), ((), ())), preferred_element_type=F32)


def _dot_tn(a, b):
    return lax.dot_general(a, b, (((0,), (0,)), ((), ())), preferred_element_type=F32)


def _split3(x):
    hi = x.astype(BF16)
    r1 = x - hi.astype(F32)
    mid = r1.astype(BF16)
    lo = (r1 - mid.astype(F32)).astype(BF16)
    return hi, mid, lo


def _ffn_body(h_ref, nw_ref, wg_ref, wu_ref, wd_ref, o_ref, hn_ref):
    @pl.when(pl.program_id(1) == 0)
    def _():
        x = h_ref[...]
        hn_ref[...] = _rms(x, nw_ref[...]).astype(BF16)
        o_ref[...] = x

    hn = hn_ref[...]
    g = _dot(hn, wg_ref[...])
    u = _dot(hn, wu_ref[...])
    a = (g * jax.nn.sigmoid(g) * u).astype(BF16)
    o_ref[...] += 0.5 * _dot(a, wd_ref[...])


def _ffn(h, nw, wg, wu, wd):
    n, d = h.shape
    fp = wg.shape[1]
    tm = min(FFN_TM, n)
    return pl.pallas_call(
        _ffn_body,
        grid=(n // tm, fp // FFN_TF),
        in_specs=[
            pl.BlockSpec((tm, d), lambda i, j: (i, 0)),
            pl.BlockSpec((1, d), lambda i, j: (0, 0)),
            pl.BlockSpec((d, FFN_TF), lambda i, j: (0, j)),
            pl.BlockSpec((d, FFN_TF), lambda i, j: (0, j)),
            pl.BlockSpec((FFN_TF, d), lambda i, j: (j, 0)),
        ],
        out_specs=pl.BlockSpec((tm, d), lambda i, j: (i, 0)),
        out_shape=jax.ShapeDtypeStruct((n, d), F32),
        scratch_shapes=[pltpu.VMEM((tm, d), BF16)],
        compiler_params=_params(("arbitrary", "arbitrary")),
    )(h, nw, wg, wu, wd)


def _norm_proj_body(h_ref, nw_ref, w_ref, o_ref, hn_ref):
    @pl.when(pl.program_id(1) == 0)
    def _():
        hn_ref[...] = _rms(h_ref[...], nw_ref[...]).astype(BF16)

    o_ref[...] = _dot(hn_ref[...], w_ref[...])


def _norm_proj(h, nw, w):
    n, d = h.shape
    c = w.shape[1]
    tm = min(PROJ_TM, n)
    tn = min(PROJ_TN, c)
    return pl.pallas_call(
        _norm_proj_body,
        grid=(n // tm, c // tn),
        in_specs=[
            pl.BlockSpec((tm, d), lambda i, j: (i, 0)),
            pl.BlockSpec((1, d), lambda i, j: (0, 0)),
            pl.BlockSpec((d, tn), lambda i, j: (0, j)),
        ],
        out_specs=pl.BlockSpec((tm, tn), lambda i, j: (i, j)),
        out_shape=jax.ShapeDtypeStruct((n, c), F32),
        scratch_shapes=[pltpu.VMEM((tm, d), BF16)],
        compiler_params=_params(("arbitrary", "arbitrary")),
    )(h, nw, w)


def _out_proj_body(n_in, h_ref, *refs):
    y_refs, w_refs, o_ref = refs[:n_in], refs[n_in:2 * n_in], refs[2 * n_in]
    acc = h_ref[...]
    for y_ref, w_ref in zip(y_refs, w_refs):
        acc = acc + _dot(y_ref[...], w_ref[...])
    o_ref[...] = acc


def _out_proj(h, ys, ws):
    n, d = h.shape
    tm = min(OUT_TM, n)
    in_specs = [pl.BlockSpec((tm, d), lambda i: (i, 0))]
    in_specs += [pl.BlockSpec((tm, y.shape[1]), lambda i: (i, 0)) for y in ys]
    in_specs += [pl.BlockSpec(w.shape, lambda i: (0, 0)) for w in ws]
    return pl.pallas_call(
        functools.partial(_out_proj_body, len(ys)),
        grid=(n // tm,),
        in_specs=in_specs,
        out_specs=pl.BlockSpec((tm, d), lambda i: (i, 0)),
        out_shape=jax.ShapeDtypeStruct((n, d), F32),
        compiler_params=_params(("arbitrary",)),
    )(h, *ys, *ws)


def _mlstm_body(bias_ref, u_ref, v_ref, og_ref, ig_ref, fg_ref, cw_ref, cb_ref, wq_ref, wk_ref, nw_ref, sk_ref,
                y_ref, ct_ref, n_ref, m_ref, up_ref):
    hh = pl.program_id(1)
    L, dh = u_ref.shape[1], u_ref.shape[2]

    @pl.when(pl.program_id(2) == 0)
    def _():
        ct_ref[...] = jnp.zeros_like(ct_ref)
        n_ref[...] = jnp.zeros_like(n_ref)
        m_ref[...] = jnp.zeros_like(m_ref)
        up_ref[...] = jnp.zeros_like(up_ref)

    u = u_ref[0]
    up = up_ref[...]
    row = lax.broadcasted_iota(jnp.int32, (L, dh), 0)
    cw = cw_ref[...]
    acc = u * cw[ML_CONV - 1:ML_CONV, :] + cb_ref[...]
    for j in range(1, ML_CONV):
        shifted = jnp.where(row < j, pltpu.roll(up, j, 0), pltpu.roll(u, j, 0))
        acc = acc + shifted * cw[ML_CONV - 1 - j:ML_CONV - j, :]
    up_ref[...] = u
    c = acc * jax.nn.sigmoid(acc)
    c16 = c.astype(BF16)
    q = _dot(c16, wq_ref[0])
    k = _dot(c16, wk_ref[0]) * (dh ** -0.5)
    q16, k16, v16 = q.astype(BF16), k.astype(BF16), v_ref[0].astype(BF16)

    f_row = _log_sigmoid(fg_ref[0, 0] + bias_ref[ML_HEADS + hh])
    i_row = ig_ref[0, 0] + bias_ref[hh]
    li = lax.broadcasted_iota(jnp.int32, (L, L), 0)
    si = lax.broadcasted_iota(jnp.int32, (L, L), 1)
    tri = si <= li
    eye = si == li
    f_b = jnp.broadcast_to(f_row, (L, L))
    i_b = jnp.broadcast_to(i_row, (L, L))
    b_col = jnp.sum(jnp.where(tri, f_b, 0.0), axis=1, keepdims=True)
    f_col = jnp.sum(jnp.where(eye, f_b, 0.0), axis=1, keepdims=True)
    i_col = jnp.sum(jnp.where(eye, i_b, 0.0), axis=1, keepdims=True)
    b_row = jnp.sum(jnp.where(li <= si, jnp.broadcast_to(f_col, (L, L)), 0.0), axis=0, keepdims=True)
    a_row = i_row - b_row
    a_col = i_col - b_col
    m_old = m_ref[...]
    mrow = jnp.maximum(m_old, jnp.max(jnp.where(tri, jnp.broadcast_to(a_row, (L, L)), NEG), axis=1, keepdims=True))
    w_intra = jnp.exp(jnp.where(tri, a_row - mrow, NEG))
    w_inter = jnp.exp(m_old - mrow)
    qk = _dot_nt(q16, k16) * w_intra
    num = _dot(qk.astype(BF16), v16) + w_inter * _dot(q16, ct_ref[...].astype(BF16))
    den = jnp.sum(qk, axis=1, keepdims=True) + w_inter * jnp.sum(q * n_ref[...], axis=1, keepdims=True)
    hout = num / jnp.maximum(jnp.abs(den), jnp.exp(-(b_col + mrow)))

    amax = jnp.maximum(m_old, jnp.max(a_row, axis=1, keepdims=True))
    g = jnp.sum(f_row, axis=1, keepdims=True)
    a_prev = jnp.exp(m_old - amax)
    ks = k * jnp.exp(a_col - amax)
    ct_ref[...] = a_prev * ct_ref[...] + _dot_tn(ks.astype(BF16), v16)
    n_ref[...] = a_prev * n_ref[...] + jnp.sum(ks, axis=0, keepdims=True)
    m_ref[...] = g + amax

    hm = _rms(hout, nw_ref[...])
    y_ref[0] = (jax.nn.sigmoid(og_ref[0]) * (hm + sk_ref[...] * c)).astype(BF16)


def _mlstm(proj, gates, gate_bias, conv_w, conv_b, wq, wk, norm_w, skip):
    b, t, _ = proj.shape
    L, dh, nh = ML_CHUNK, ML_HEAD_DIM, ML_HEADS
    vec = lambda: pl.BlockSpec((1, dh), lambda bi, hi, ci: (0, hi))
    return pl.pallas_call(
        _mlstm_body,
        grid=(b, nh, t // L),
        in_specs=[
            pl.BlockSpec(memory_space=pltpu.SMEM),
            pl.BlockSpec((1, L, dh), lambda bi, hi, ci: (bi, ci, hi)),
            pl.BlockSpec((1, L, dh), lambda bi, hi, ci: (bi, ci, nh + hi)),
            pl.BlockSpec((1, L, dh), lambda bi, hi, ci: (bi, ci, 2 * nh + hi)),
            pl.BlockSpec((1, 1, 1, L), lambda bi, hi, ci: (bi, hi, 0, ci)),
            pl.BlockSpec((1, 1, 1, L), lambda bi, hi, ci: (bi, nh + hi, 0, ci)),
            pl.BlockSpec((ML_CONV, dh), lambda bi, hi, ci: (0, hi)),
            vec(),
            pl.BlockSpec((1, dh, dh), lambda bi, hi, ci: (hi, 0, 0)),
            pl.BlockSpec((1, dh, dh), lambda bi, hi, ci: (hi, 0, 0)),
            vec(),
            vec(),
        ],
        out_specs=pl.BlockSpec((1, L, dh), lambda bi, hi, ci: (bi, ci, hi)),
        out_shape=jax.ShapeDtypeStruct((b, t, ML_WIDTH), BF16),
        scratch_shapes=[
            pltpu.VMEM((dh, dh), F32),
            pltpu.VMEM((1, dh), F32),
            pltpu.VMEM((1, 1), F32),
            pltpu.VMEM((L, dh), F32),
        ],
        compiler_params=_params(("arbitrary", "arbitrary", "arbitrary")),
    )(gate_bias, proj, proj, proj, gates, gates, conv_w, conv_b, wq, wk, norm_w, skip)


def _hg_level_tables(L):
    l = np.arange(L)[:, None]
    s = np.arange(L)[None, :]
    mats = [(s <= l)]
    mats.append(s <= (l // 8) * 8 + 3)
    for h in HG_LEVELS:
        mats.append(s <= (l // (2 * h)) * (2 * h) + h - 1)
    stack = np.concatenate(mats, axis=0).astype(np.float32)
    lvl = np.full((L, L), -1.0, np.float32)
    lvl[(l // 8 == s // 8) & (s <= l)] = 0.0
    for idx, h in enumerate(HG_LEVELS):
        sel = (l // (2 * h) == s // (2 * h)) & (l % (2 * h) >= h) & (s % (2 * h) < h)
        lvl[sel] = idx + 1.0
    return stack, lvl


def _hgrn2_head(hq, z, vi, hg, lb, nw, stack, lvl, st):
    L = hq.shape[0]
    k = (1.0 - lb) * jax.nn.sigmoid(-z)
    log_f = jnp.maximum(jnp.log1p(-jnp.minimum(k, HG_MAX_K)), _log_sigmoid(z))
    q = hq * jax.nn.sigmoid(hq)
    v16 = vi.astype(BF16)

    hi, mid, lo = _split3(log_f)
    br = _dot(stack, hi) + _dot(stack, mid) + _dot(stack, lo)
    b = br[0:L]
    r0 = br[L:2 * L]
    att = jnp.where(lvl == 0.0, _dot_nt((q * jnp.exp(b - r0)).astype(BF16), (k * jnp.exp(r0 - b)).astype(BF16)), 0.0)
    for idx in range(len(HG_LEVELS)):
        r = br[(idx + 2) * L:(idx + 3) * L]
        qd = (q * jnp.exp(jnp.minimum(b - r, 0.0))).astype(BF16)
        kd = (k * jnp.exp(jnp.minimum(r - b, 0.0))).astype(BF16)
        att = att + jnp.where(lvl == idx + 1.0, _dot_nt(qd, kd), 0.0)

    o = _dot(att.astype(BF16), v16) + _dot_nt((q * jnp.exp(b)).astype(BF16), st.astype(BF16))
    g = b[L - 1:L, :]
    st_new = st * jnp.exp(g) + _dot_tn(v16, (k * jnp.exp(g - b)).astype(BF16))
    y = (_rms(o, nw) * (hg * jax.nn.sigmoid(hg))).astype(BF16)
    return y, st_new


def _hgrn2_body(q_ref, f_ref, i_ref, g_ref, lb_ref, nw_ref, stack_ref, lvl_ref, y_ref, st_ref):
    @pl.when(pl.program_id(2) == 0)
    def _():
        st_ref[...] = jnp.zeros_like(st_ref)

    stack = stack_ref[...]
    lvl = lvl_ref[...]
    dk = HG_KEY_DIM
    for hh in range(st_ref.shape[0]):
        cs = slice(hh * dk, (hh + 1) * dk)
        y, st = _hgrn2_head(q_ref[0, :, cs], f_ref[0, :, cs], i_ref[0, :, cs], g_ref[0, :, cs], lb_ref[:, cs],
                            nw_ref[:, cs], stack, lvl, st_ref[hh])
        st_ref[hh] = st
        y_ref[0, :, cs] = y


def _hgrn2(proj, lb, norm_w):
    b, t, _ = proj.shape
    L, dk, nh = HG_CHUNK, HG_KEY_DIM, HG_HEADS
    hps = HG_HEADS_PER_STEP
    wd = hps * dk
    stack, lvl = _hg_level_tables(L)
    base = 3 * ML_WIDTH // wd
    ng = nh // hps
    col = lambda off: pl.BlockSpec((1, L, wd), lambda bi, hi, ci: (bi, ci, base + off * ng + hi))
    vec = lambda: pl.BlockSpec((1, wd), lambda bi, hi, ci: (0, hi))
    return pl.pallas_call(
        _hgrn2_body,
        grid=(b, ng, t // L),
        in_specs=[
            col(0), col(1), col(2), col(3), vec(), vec(),
            pl.BlockSpec(stack.shape, lambda bi, hi, ci: (0, 0)),
            pl.BlockSpec(lvl.shape, lambda bi, hi, ci: (0, 0)),
        ],
        out_specs=pl.BlockSpec((1, L, wd), lambda bi, hi, ci: (bi, ci, hi)),
        out_shape=jax.ShapeDtypeStruct((b, t, HG_WIDTH), BF16),
        scratch_shapes=[pltpu.VMEM((hps, HG_VAL_DIM, dk), F32)],
        compiler_params=_params(("arbitrary", "arbitrary", "arbitrary")),
    )(proj, proj, proj, proj, lb, norm_w, jnp.asarray(stack, BF16), jnp.asarray(lvl))


def _rope_tables(pos):
    half = NSA_HEAD_DIM // 2
    inv_freq = jnp.power(ROPE_THETA, -jnp.arange(half, dtype=F32) / half)
    ang = pos.astype(F32)[:, None] * inv_freq[None, :]
    cos, sin = jnp.cos(ang), jnp.sin(ang)
    return jnp.concatenate([cos, cos], axis=1), jnp.concatenate([-sin, sin], axis=1)


def _rope(x, cos, sin_signed):
    return x * cos + pltpu.roll(x, NSA_HEAD_DIM // 2, 1) * sin_signed


def _head_prep_body(scale, x_ref, nw_ref, cos_ref, sin_ref, o_ref):
    y = _rope(_rms(x_ref[0], nw_ref[...]), cos_ref[...], sin_ref[...])
    o_ref[0, 0] = (y * scale).astype(BF16)


def _head_prep(proj, col0, nheads, nw, cos, sin, scale):
    b, t, _ = proj.shape
    d = NSA_HEAD_DIM
    tt = min(PREP_TT, t)
    return pl.pallas_call(
        functools.partial(_head_prep_body, scale),
        grid=(b, nheads, t // tt),
        in_specs=[
            pl.BlockSpec((1, tt, d), lambda bi, hi, ti: (bi, ti, col0 + hi)),
            pl.BlockSpec((1, d), lambda bi, hi, ti: (0, 0)),
            pl.BlockSpec((tt, d), lambda bi, hi, ti: (ti, 0)),
            pl.BlockSpec((tt, d), lambda bi, hi, ti: (ti, 0)),
        ],
        out_specs=pl.BlockSpec((1, 1, tt, d), lambda bi, hi, ti: (bi, hi, ti, 0)),
        out_shape=jax.ShapeDtypeStruct((b, nheads, t, d), BF16),
        compiler_params=_params(("arbitrary", "arbitrary", "arbitrary")),
    )(proj, nw, cos, sin)


def _compress_body(do_norm, x_ref, pt_ref, pb_ref, w1t_ref, w1b_ref, b1_ref, w2_ref, nw_ref, cos_ref, sin_ref, o_ref):
    x = x_ref[0, 0]
    nh = x.shape[0]
    top = _dot((x + pt_ref[0]).astype(BF16), w1t_ref[0])
    bot = _dot((x + pb_ref[0]).astype(BF16), w1b_ref[0])
    hid = top + pltpu.roll(bot, nh - 1, 0) + b1_ref[0]
    hid = hid * jax.nn.sigmoid(hid)
    y = _dot(hid.astype(BF16), w2_ref[0])
    if do_norm:
        y = _rope(_rms(y, nw_ref[...]), cos_ref[...], sin_ref[...])
    o_ref[0, 0] = y.astype(BF16)


def _compress(x, which, do_norm, pos, w1, b1, w2, nw, cos, sin):
    b, g, nh, wd = x.shape
    d = NSA_HEAD_DIM
    half = CMP_STRIDE * d
    pos_t = pos[:, :CMP_STRIDE].reshape(2, 1, half)
    pos_b = pos[:, CMP_STRIDE:].reshape(2, 1, half)
    w1t = w1[:, :half].astype(BF16)
    w1b = w1[:, half:].astype(BF16)
    sel = lambda bi, gi: (which, 0, 0)
    return pl.pallas_call(
        functools.partial(_compress_body, do_norm),
        grid=(b, g),
        in_specs=[
            pl.BlockSpec((1, 1, nh, wd), lambda bi, gi: (bi, gi, 0, 0)),
            pl.BlockSpec((1, 1, half), sel),
            pl.BlockSpec((1, 1, half), sel),
            pl.BlockSpec((1, half, CMP_HIDDEN), sel),
            pl.BlockSpec((1, half, CMP_HIDDEN), sel),
            pl.BlockSpec((1, 1, CMP_HIDDEN), sel),
            pl.BlockSpec((1, CMP_HIDDEN, d), sel),
            pl.BlockSpec((1, d), lambda bi, gi: (0, 0)),
            pl.BlockSpec((nh, d), lambda bi, gi: (0, 0)),
            pl.BlockSpec((nh, d), lambda bi, gi: (0, 0)),
        ],
        out_specs=pl.BlockSpec((1, 1, nh, d), lambda bi, gi: (bi, gi, 0, 0)),
        out_shape=jax.ShapeDtypeStruct((b, g, nh, d), BF16),
        compiler_params=_params(("arbitrary", "arbitrary")),
    )(x, pos_t, pos_b, w1t, w1b, b1.reshape(2, 1, CMP_HIDDEN), w2.astype(BF16), nw, cos, sin)


def _nsa_cmp_body(q_ref, kc_ref, vc_ref, cov_ref, oc_ref, pen_ref, imp_ref):
    R, TQ, d = q_ref.shape[2], q_ref.shape[3], q_ref.shape[4]
    ncp = kc_ref.shape[2]
    ns = cov_ref.shape[0]
    t0 = pl.program_id(2) * TQ
    q4 = q_ref[0, 0].reshape(R * TQ, d)
    s = _dot_nt(kc_ref[0, 0], q4)
    n_i = lax.broadcasted_iota(jnp.int32, (ncp, R * TQ), 0)
    t_i = t0 + (lax.broadcasted_iota(jnp.int32, (ncp, R * TQ), 1) & (TQ - 1))
    valid = n_i * CMP_STRIDE + (CMP_BLOCK - 1) <= t_i
    s = jnp.where(valid, s, NEG)
    e = jnp.where(valid, jnp.exp(s - jnp.max(s, axis=0, keepdims=True)), 0.0)
    zsum = jnp.sum(e, axis=0, keepdims=True)
    p = e / jnp.where(zsum > 0.0, zsum, 1.0)
    oc_ref[0, 0] = _dot_tn(p.astype(BF16), vc_ref[0, 0]).reshape(R, TQ, d)

    psum = p[:, 0:TQ]
    for r in range(1, R):
        psum = psum + p[:, r * TQ:(r + 1) * TQ]
    p_hi = psum.astype(BF16)
    p_lo = (psum - p_hi.astype(F32)).astype(BF16)
    cov = cov_ref[...]
    imp = _dot(cov, p_hi) + _dot(cov, p_lo)
    blk = lax.broadcasted_iota(jnp.int32, (ns, TQ), 0)
    tt = t0 + lax.broadcasted_iota(jnp.int32, (ns, TQ), 1)
    cur = lax.shift_right_logical(tt, int(np.log2(SEL_BLOCK)))
    forced = (blk == 0) | (blk == cur) | (blk == cur - 1)
    imp = jnp.where(blk * SEL_BLOCK <= tt, imp + jnp.where(forced, FORCE_BONUS, 0.0), NEG)
    imp_ref[...] = imp

    def count(j, rank):
        x = imp_ref[pl.ds(j, 1), :]
        ge = jnp.where(x >= imp, 1.0, 0.0)
        gt = jnp.where(x > imp, 1.0, 0.0)
        return rank + jnp.where(blk > j, ge, gt)

    n_causal = jnp.minimum((t0 + TQ + SEL_BLOCK - 1) // SEL_BLOCK, ns)
    rank = lax.fori_loop(0, n_causal, count, jnp.zeros((ns, TQ), F32))
    pen_t = jnp.where(rank < float(min(SEL_TOPK, ns)), 0.0, NEG)
    pen_ref[0, 0] = pen_t.T.astype(BF16)


def _nsa_cmp(q, k_cmp, v_cmp):
    b, g, r, t, d = q.shape
    ncp = k_cmp.shape[2]
    ns = t // SEL_BLOCK
    tq = min(NSA_CMP_TQ, t)
    ci = np.arange(ncp)[None, :] * CMP_STRIDE
    sj = np.arange(ns)[:, None] * SEL_BLOCK
    cover_t = ((ci < sj + SEL_BLOCK) & (ci + CMP_BLOCK > sj) & (np.arange(ncp)[None, :] < ncp - 1))
    return pl.pallas_call(
        _nsa_cmp_body,
        grid=(b, g, t // tq),
        in_specs=[
            pl.BlockSpec((1, 1, r, tq, d), lambda bi, gi, qi: (bi, gi, 0, qi, 0)),
            pl.BlockSpec((1, 1, ncp, d), lambda bi, gi, qi: (bi, gi, 0, 0)),
            pl.BlockSpec((1, 1, ncp, d), lambda bi, gi, qi: (bi, gi, 0, 0)),
            pl.BlockSpec((ns, ncp), lambda bi, gi, qi: (0, 0)),
        ],
        out_specs=[
            pl.BlockSpec((1, 1, r, tq, d), lambda bi, gi, qi: (bi, gi, 0, qi, 0)),
            pl.BlockSpec((1, 1, tq, ns), lambda bi, gi, qi: (bi, gi, qi, 0)),
        ],
        out_shape=[
            jax.ShapeDtypeStruct((b, g, r, t, d), F32),
            jax.ShapeDtypeStruct((b, g, t, ns), BF16),
        ],
        scratch_shapes=[pltpu.VMEM((ns, tq), F32)],
        compiler_params=_params(("arbitrary", "arbitrary", "arbitrary")),
    )(q, k_cmp, v_cmp, jnp.asarray(cover_t.astype(np.float32), BF16))


def _flash_step(s, v16, m_ref, l_ref, acc_ref):
    m_prev = m_ref[...]
    m_next = jnp.maximum(m_prev, jnp.max(functools.reduce(jnp.maximum, s), axis=1, keepdims=True))
    alpha = jnp.exp(m_prev - m_next)
    p = [jnp.exp(sc - m_next) for sc in s]
    l_ref[...] = alpha * l_ref[...] + jnp.sum(functools.reduce(jnp.add, p), axis=1, keepdims=True)
    p16 = jnp.concatenate([pc.astype(BF16) for pc in p], axis=1)
    acc_ref[...] = alpha * acc_ref[...] + _dot(p16, v16)
    m_ref[...] = m_next


def _col_groups(s):
    return [s[:, c * LANES:(c + 1) * LANES] for c in range(s.shape[1] // LANES)]


def _nsa_sel_body(q_ref, pen_ref, ks_ref, e_ref, vs_ref, kw_ref, vw_ref, oc_ref, gp_ref, gb_ref, o_ref,
                  m1_ref, l1_ref, a1_ref, m2_ref, l2_ref, a2_ref):
    R, TQ, d = q_ref.shape[2], q_ref.shape[3], q_ref.shape[4]
    rows = R * TQ
    qi = pl.program_id(2)
    q4 = q_ref[0, 0].reshape(rows, d)
    pen = pen_ref[0, 0]
    qa = jnp.concatenate([q4, jnp.concatenate([pen] * R, axis=0)], axis=1)
    t_rel = lax.broadcasted_iota(jnp.int32, (rows, LANES), 0) & (TQ - 1)
    lane = lax.broadcasted_iota(jnp.int32, (rows, LANES), 1)

    for m_ref, l_ref, a_ref in ((m1_ref, l1_ref, a1_ref), (m2_ref, l2_ref, a2_ref)):
        m_ref[...] = jnp.full_like(m_ref, NEG)
        l_ref[...] = jnp.zeros_like(l_ref)
        a_ref[...] = jnp.zeros_like(a_ref)

    def sel_scores(kt):
        k0 = pl.multiple_of(kt * TQ, TQ)
        ka = jnp.concatenate([ks_ref[0, 0, pl.ds(k0, TQ), :], e_ref[pl.ds(k0, TQ), :]], axis=1)
        return _col_groups(_dot_nt(qa, ka)), vs_ref[0, 0, pl.ds(k0, TQ), :]

    def win_scores(kt):
        k0 = pl.multiple_of(kt * TQ, TQ)
        return _col_groups(_dot_nt(q4, kw_ref[0, 0, pl.ds(k0, TQ), :])), vw_ref[0, 0, pl.ds(k0, TQ), :]

    def causal(s):
        return [jnp.where(c * LANES + lane <= t_rel, sc, NEG) for c, sc in enumerate(s)]

    s, v = sel_scores(qi)
    _flash_step(causal(s), v, m1_ref, l1_ref, a1_ref)

    def sel_pair(i, carry):
        sa, va = sel_scores(2 * i)
        sb, vb = sel_scores(2 * i + 1)
        _flash_step(sa, va, m1_ref, l1_ref, a1_ref)
        _flash_step(sb, vb, m1_ref, l1_ref, a1_ref)
        return carry

    lax.fori_loop(0, lax.shift_right_logical(qi, 1), sel_pair, 0)

    @pl.when((qi & 1) == 1)
    def _():
        s, v = sel_scores(qi - 1)
        _flash_step(s, v, m1_ref, l1_ref, a1_ref)

    s, v = win_scores(qi)
    _flash_step(causal(s), v, m2_ref, l2_ref, a2_ref)
    n_back = WINDOW // TQ

    def win_step(kt, carry):
        s, v = win_scores(kt)
        _flash_step(s, v, m2_ref, l2_ref, a2_ref)
        return carry

    lax.fori_loop(jnp.maximum(qi - n_back + 1, 0), qi, win_step, 0)

    @pl.when(qi >= n_back)
    def _():
        s, v = win_scores(qi - n_back)
        s = [jnp.where(c * LANES + lane > t_rel, sc, NEG) for c, sc in enumerate(s)]
        _flash_step(s, v, m2_ref, l2_ref, a2_ref)

    gate = jax.nn.sigmoid(gp_ref[0, 0] + gb_ref[0])
    o_sel = a1_ref[...] / l1_ref[...]
    o_win = a2_ref[...] / l2_ref[...]
    for r in range(R):
        rs = slice(r * TQ, (r + 1) * TQ)
        o = (gate[:, 3 * r:3 * r + 1] * oc_ref[0, 0, r]
             + gate[:, 3 * r + 1:3 * r + 2] * o_sel[rs]
             + gate[:, 3 * r + 2:3 * r + 3] * o_win[rs])
        o_ref[0, :, r * d:(r + 1) * d] = o.astype(BF16)


def _nsa_sel(q, pen, k_sel, v_sel, k_win, v_win, o_cmp, gate_pre, gate_bias):
    b, g, r, t, d = q.shape
    ns = pen.shape[3]
    tq = min(NSA_TQ, t)
    assert WINDOW % tq == 0 and t % tq == 0 and d == LANES
    rows = r * tq
    onehot = (np.arange(t)[:, None] // SEL_BLOCK == np.arange(ns)[None, :]).astype(np.float32)
    kv = lambda: pl.BlockSpec((1, 1, t, d), lambda bi, gi, qi: (bi, gi, 0, 0))
    qspec = lambda: pl.BlockSpec((1, 1, r, tq, d), lambda bi, gi, qi: (bi, gi, 0, qi, 0))
    gw = gate_pre.shape[3]
    stat = lambda: pltpu.VMEM((rows, LANES), F32)
    return pl.pallas_call(
        _nsa_sel_body,
        grid=(b, g, t // tq),
        in_specs=[
            qspec(),
            pl.BlockSpec((1, 1, tq, ns), lambda bi, gi, qi: (bi, gi, qi, 0)),
            kv(),
            pl.BlockSpec((t, ns), lambda bi, gi, qi: (0, 0)),
            kv(), kv(), kv(),
            qspec(),
            pl.BlockSpec((1, 1, tq, gw), lambda bi, gi, qi: (bi, gi, qi, 0)),
            pl.BlockSpec((1, 1, gw), lambda bi, gi, qi: (gi, 0, 0)),
        ],
        out_specs=pl.BlockSpec((1, tq, r * d), lambda bi, gi, qi: (bi, qi, gi)),
        out_shape=jax.ShapeDtypeStruct((b, t, g * r * d), BF16),
        scratch_shapes=[stat(), stat(), stat(), stat(), stat(), stat()],
        compiler_params=_params(("arbitrary", "arbitrary", "arbitrary")),
    )(q, pen, k_sel, jnp.asarray(onehot, BF16), v_sel, k_win, v_win, o_cmp, gate_pre, gate_bias)


def _pad_cols(w, mult):
    pad = (-w.shape[1]) % mult
    return jnp.pad(w, ((0, 0), (0, pad))) if pad else w


def _ab_mixer(h, bsz, nw, w_in, w_out, conv_w, conv_b, wq, wk, i_bias, f_bias, ml_norm, ml_skip, lb, hg_norm):
    n, _ = h.shape
    t = n // bsz
    g0 = 3 * ML_WIDTH
    g1 = g0 + 2 * ML_HEADS
    w_main = jnp.concatenate([w_in[:, :g0], w_in[:, g1:]], axis=1).astype(BF16)
    w_gate = _pad_cols(w_in[:, g0:g1], LANES).astype(BF16)
    proj = _norm_proj(h, nw, w_main).reshape(bsz, t, -1)
    gates = _norm_proj(h, nw, w_gate)[:, :2 * ML_HEADS]
    gates = gates.reshape(bsz, t, 2 * ML_HEADS).transpose(0, 2, 1).reshape(bsz, 2 * ML_HEADS, 1, t)
    y_ml = _mlstm(proj, gates, jnp.concatenate([i_bias, f_bias]), conv_w, conv_b.reshape(1, -1),
                  wq.astype(BF16), wk.astype(BF16), ml_norm.reshape(1, -1), ml_skip.reshape(1, -1))
    y_hg = _hgrn2(proj, lb.reshape(1, -1), hg_norm.reshape(1, -1))
    w_out16 = w_out.astype(BF16)
    return _out_proj(h, [y_ml.reshape(n, -1), y_hg.reshape(n, -1)], [w_out16[:ML_WIDTH], w_out16[ML_WIDTH:]])


def _nsa_mixer(h, bsz, nw, w_in, w_out, q_norm, k_norm, cmp_pos, cmp_w1, cmp_b1, cmp_w2, gate_bias):
    n, _ = h.shape
    t = n // bsz
    d, G, R = NSA_HEAD_DIM, NSA_KV_GROUPS, NSA_REP
    c_main = NSA_HEADS * d + 6 * NSA_KV_WIDTH
    proj = _norm_proj(h, nw, w_in[:, :c_main].astype(BF16)).reshape(bsz, t, c_main)
    gp = _norm_proj(h, nw, _pad_cols(w_in[:, c_main:], LANES).astype(BF16))[:, :3 * NSA_HEADS]

    cos, sin = _rope_tables(jnp.arange(t))
    nh = t // CMP_STRIDE
    cos_c, sin_c = _rope_tables(jnp.arange(nh) * CMP_STRIDE + CMP_BLOCK - 1)
    q = _head_prep(proj, 0, NSA_HEADS, q_norm.reshape(1, d), cos, sin, d ** -0.5).reshape(bsz, G, R, t, d)
    kvb = NSA_HEADS
    k_sel = _head_prep(proj, kvb + 2 * G, G, k_norm[1].reshape(1, d), cos, sin, 1.0)
    k_win = _head_prep(proj, kvb + 4 * G, G, k_norm[2].reshape(1, d), cos, sin, 1.0)

    def kv_heads(idx):
        c0 = NSA_HEADS * d + idx * NSA_KV_WIDTH
        return proj[:, :, c0:c0 + NSA_KV_WIDTH].reshape(bsz, t, G, d).transpose(0, 2, 1, 3)

    v_sel = kv_heads(3).astype(BF16)
    v_win = kv_heads(5).astype(BF16)
    k_cmp = _compress(kv_heads(0).reshape(bsz, G, nh, CMP_STRIDE * d), 0, True, cmp_pos, cmp_w1, cmp_b1, cmp_w2,
                      k_norm[0].reshape(1, d), cos_c, sin_c)
    v_cmp = _compress(kv_heads(1).reshape(bsz, G, nh, CMP_STRIDE * d), 1, False, cmp_pos, cmp_w1, cmp_b1, cmp_w2,
                      k_norm[0].reshape(1, d), cos_c, sin_c)
    o_cmp, pen = _nsa_cmp(q, k_cmp, v_cmp)

    gw = 16
    gp = jnp.pad(gp.reshape(bsz, t, G, 3 * R), ((0, 0), (0, 0), (0, 0), (0, gw - 3 * R))).transpose(0, 2, 1, 3)
    gb = jnp.pad(gate_bias.reshape(G, 1, 3 * R), ((0, 0), (0, 0), (0, gw - 3 * R)))
    o = _nsa_sel(q, pen, k_sel, v_sel, k_win, v_win, o_cmp, gp, gb)
    return _out_proj(h, [o.reshape(n, -1)], [w_out.astype(BF16)])


def _ffn_weights(wg, wu, wd):
    pad = D_FF_PAD - D_FF
    wg = jnp.pad(wg, ((0, 0), (0, pad))).astype(BF16)
    wu = jnp.pad(wu, ((0, 0), (0, pad))).astype(BF16)
    wd = jnp.pad(wd, ((0, pad), (0, 0))).astype(BF16)
    return wg, wu, wd


def kernel(x, ffn1_norm, ffn1_w_gate, ffn1_w_up, ffn1_w_down, mix_norm, ffn2_norm, ffn2_w_gate, ffn2_w_up,
           ffn2_w_down, ab_w_in, ab_w_out, ml_conv_w, ml_conv_b, ml_wq, ml_wk, ml_i_bias, ml_f_bias, ml_out_norm,
           ml_skip, hg_lb_logits, hg_out_norm, c_w_in, c_w_out, c_q_norm, c_k_norm, c_cmp_pos, c_cmp_w1, c_cmp_b1,
           c_cmp_w2, c_gate_bias):
    bsz, t, d = x.shape
    depth = ffn1_norm.shape[0]
    lb_soft = jax.nn.softmax(hg_lb_logits.astype(F32), axis=0)
    lb_all = jnp.cumsum(lb_soft, axis=0) - lb_soft[0]
    h = x.reshape(bsz * t, d)
    for layer in range(depth):
        h = _ffn(h, ffn1_norm[layer].reshape(1, d),
                 *_ffn_weights(ffn1_w_gate[layer], ffn1_w_up[layer], ffn1_w_down[layer]))
        j = layer // 2
        nw = mix_norm[layer].reshape(1, d)
        if layer % 2 == 0:
            h = _ab_mixer(h, bsz, nw, ab_w_in[j], ab_w_out[j], ml_conv_w[j], ml_conv_b[j], ml_wq[j], ml_wk[j],
                          ml_i_bias[j], ml_f_bias[j], ml_out_norm[j], ml_skip[j], lb_all[j], hg_out_norm[j])
        else:
            h = _nsa_mixer(h, bsz, nw, c_w_in[j], c_w_out[j], c_q_norm[j], c_k_norm[j], c_cmp_pos[j], c_cmp_w1[j],
                           c_cmp_b1[j], c_cmp_w2[j], c_gate_bias[j])
        h = _ffn(h, ffn2_norm[layer].reshape(1, d),
                 *_ffn_weights(ffn2_w_gate[layer], ffn2_w_up[layer], ffn2_w_down[layer]))
    return h.reshape(bsz, t, d)
```
